```python
import math
import jax, jax.numpy as jnp
from jax import lax
import numpy as np

D_MODEL = 1024
BATCH = 32
SEQ = 2048
DEPTH = 2

D_MIX = D_MODEL
SSM_WIDTH = D_MIX // 2
SSM_CH = 16
SSM_GROUPS = SSM_WIDTH // SSM_CH
SSM_STATE = 64
GM_WIDTH = D_MIX - SSM_WIDTH
GM_HEADS = 4
GM_HEAD_DIM = GM_WIDTH // GM_HEADS
GM_CHUNK = 128
D_FF = ((8 * D_MODEL // 3 + 127) // 128) * 128
IN_COLS = SSM_WIDTH + 2 * GM_WIDTH
EPS = 1e-6
DT_MIN = 1e-3
DT_MAX = 1e-1

kernel_name = "hybrid_s5_gmlp_macaron"


def rmsnorm(x, g):
    xf = x.astype(jnp.float32)
    y = xf * lax.rsqrt(jnp.mean(xf * xf, axis=-1, keepdims=True) + EPS)
    return (y * g.astype(jnp.float32)).astype(x.dtype)


def swiglu(h, w_in, w_out):
    gu = h @ w_in
    g, u = gu[..., :D_FF], gu[..., D_FF:]
    return (jax.nn.silu(g) * u) @ w_out


def s5_group(u, a_re, a_im, log_dt, b_re, b_im, c_re, c_im, d_skip, glu_w, glu_b):
    bsz, seq = u.shape[0], u.shape[1]
    f32 = jnp.float32
    ug = u.reshape(bsz, seq, SSM_GROUPS, SSM_CH).astype(f32)
    lam = lax.complex(a_re.astype(f32), a_im.astype(f32))
    dt = jnp.exp(log_dt.astype(f32))[:, None]
    lam_bar = jnp.exp(lam * dt)
    b_mat = lax.complex(b_re.astype(f32), b_im.astype(f32))
    b_bar = ((lam_bar - 1.0) / lam)[..., None] * b_mat
    bu = jnp.einsum('blgc,gpc->blgp', ug, b_bar)
    a_elems = jnp.broadcast_to(lam_bar, (seq, SSM_GROUPS, SSM_STATE))

    def combine(left, right):
        a_l, b_l = left
        a_r, b_r = right
        return a_r * a_l, a_r * b_l + b_r

    def scan_one(bu_b):
        _, h = lax.associative_scan(combine, (a_elems, bu_b), axis=0)
        return h

    h = jax.vmap(scan_one)(bu)
    c_mat = lax.complex(c_re.astype(f32), c_im.astype(f32))
    y = jnp.einsum('blgp,gcp->blgc', h, c_mat).real + d_skip.astype(f32) * ug
    y = jax.nn.gelu(y)
    z = jnp.einsum('blgc,gce->blge', y, glu_w.astype(f32)) + glu_b.astype(f32)
    out = z[..., :SSM_CH] * jax.nn.sigmoid(z[..., SSM_CH:])
    return out.reshape(bsz, seq, SSM_WIDTH).astype(u.dtype)


def gmlp_group(u, v, v_gain, w_s, b_s):
    bsz, seq = u.shape[0], u.shape[1]
    u = jax.nn.gelu(u)
    v = jax.nn.gelu(v).reshape(bsz, seq // GM_CHUNK, GM_CHUNK, GM_HEADS, GM_HEAD_DIM)
    v = rmsnorm(v, v_gain.reshape(GM_HEADS, GM_HEAD_DIM))
    mask = jnp.tril(jnp.ones((GM_CHUNK, GM_CHUNK), dtype=bool))
    ws = jnp.where(mask[None], w_s, jnp.zeros_like(w_s))
    s = jnp.einsum('hts,bnshd->bnthd', ws, v) + b_s.T[None, None, :, :, None]
    return u * s.reshape(bsz, seq, GM_WIDTH)


def setup_inputs(seed: int = 0) -> dict:
    key = jax.random.key(seed)
    ks = jax.random.split(key, 26)
    f32 = jnp.float32
    nrm = lambda k, shape, scale: scale * jax.random.normal(k, shape, f32)
    gain = lambda k, shape: 1.0 + 0.05 * jax.random.normal(k, shape, f32)
    n_idx = jnp.arange(SSM_STATE, dtype=f32)
    return {
        "x": jax.random.normal(ks[0], (BATCH, SEQ, D_MODEL), f32),
        "norm_ffn1": gain(ks[1], (DEPTH, D_MODEL)),
        "ffn1_w_in": nrm(ks[2], (DEPTH, D_MODEL, 2 * D_FF), D_MODEL ** -0.5),
        "ffn1_w_out": nrm(ks[3], (DEPTH, D_FF, D_MODEL), D_FF ** -0.5),
        "norm_mix": gain(ks[4], (DEPTH, D_MODEL)),
        "mix_w_in": nrm(ks[5], (DEPTH, D_MODEL, IN_COLS), D_MODEL ** -0.5),
        "ssm_a_re": -0.5 + nrm(ks[6], (DEPTH, SSM_GROUPS, SSM_STATE), 0.01),
        "ssm_a_im": math.pi * n_idx + nrm(ks[7], (DEPTH, SSM_GROUPS, SSM_STATE), 0.01),
        "ssm_log_dt": jax.random.uniform(ks[8], (DEPTH, SSM_GROUPS), f32,
                                         math.log(DT_MIN), math.log(DT_MAX)),
        "ssm_b_re": nrm(ks[9], (DEPTH, SSM_GROUPS, SSM_STATE, SSM_CH), (2 * SSM_CH) ** -0.5),
        "ssm_b_im": nrm(ks[10], (DEPTH, SSM_GROUPS, SSM_STATE, SSM_CH), (2 * SSM_CH) ** -0.5),
        "ssm_c_re": nrm(ks[11], (DEPTH, SSM_GROUPS, SSM_CH, SSM_STATE), (2 * SSM_STATE) ** -0.5),
        "ssm_c_im": nrm(ks[12], (DEPTH, SSM_GROUPS, SSM_CH, SSM_STATE), (2 * SSM_STATE) ** -0.5),
        "ssm_d": nrm(ks[13], (DEPTH, SSM_GROUPS, SSM_CH), 1.0),
        "ssm_glu_w": nrm(ks[14], (DEPTH, SSM_GROUPS, SSM_CH, 2 * SSM_CH), SSM_CH ** -0.5),
        "ssm_glu_b": nrm(ks[15], (DEPTH, SSM_GROUPS, 2 * SSM_CH), 0.02),
        "gm_v_gain": gain(ks[16], (DEPTH, GM_WIDTH)),
        "gm_w_s": nrm(ks[17], (DEPTH, GM_HEADS, GM_CHUNK, GM_CHUNK), 0.5 * GM_CHUNK ** -0.5),
        "gm_b_s": 1.0 + nrm(ks[18], (DEPTH, GM_HEADS, GM_CHUNK), 0.1),
        "gain_ssm_out": gain(ks[19], (DEPTH, SSM_WIDTH)),
        "gain_gm_out": gain(ks[20], (DEPTH, GM_WIDTH)),
        "mix_w_out": nrm(ks[21], (DEPTH, D_MIX, D_MODEL), D_MIX ** -0.5),
        "norm_ffn2": gain(ks[22], (DEPTH, D_MODEL)),
        "ffn2_w_in": nrm(ks[23], (DEPTH, D_MODEL, 2 * D_FF), D_MODEL ** -0.5),
        "ffn2_w_out": nrm(ks[24], (DEPTH, D_FF, D_MODEL), D_FF ** -0.5),
        "norm_final": gain(ks[25], (D_MODEL,)),
    }


def reference(x, norm_ffn1, ffn1_w_in, ffn1_w_out, norm_mix, mix_w_in,
              ssm_a_re, ssm_a_im, ssm_log_dt, ssm_b_re, ssm_b_im, ssm_c_re, ssm_c_im,
              ssm_d, ssm_glu_w, ssm_glu_b, gm_v_gain, gm_w_s, gm_b_s,
              gain_ssm_out, gain_gm_out, mix_w_out, norm_ffn2, ffn2_w_in, ffn2_w_out,
              norm_final):
    for l in range(DEPTH):
        x = x + 0.5 * swiglu(rmsnorm(x, norm_ffn1[l]), ffn1_w_in[l], ffn1_w_out[l])
        z = rmsnorm(x, norm_mix[l]) @ mix_w_in[l]
        u_ssm = z[..., :SSM_WIDTH]
        u_gm = z[..., SSM_WIDTH:SSM_WIDTH + GM_WIDTH]
        v_gm = z[..., SSM_WIDTH + GM_WIDTH:]
        y_ssm = s5_group(u_ssm, ssm_a_re[l], ssm_a_im[l], ssm_log_dt[l],
                         ssm_b_re[l], ssm_b_im[l], ssm_c_re[l], ssm_c_im[l],
                         ssm_d[l], ssm_glu_w[l], ssm_glu_b[l])
        y_gm = gmlp_group(u_gm, v_gm, gm_v_gain[l], gm_w_s[l], gm_b_s[l])
        y = jnp.concatenate([rmsnorm(y_ssm, gain_ssm_out[l]),
                             rmsnorm(y_gm, gain_gm_out[l])], axis=-1)
        x = x + y @ mix_w_out[l]
        x = x + 0.5 * swiglu(rmsnorm(x, norm_ffn2[l]), ffn2_w_in[l], ffn2_w_out[l])
    return rmsnorm(x, norm_final)
```

```python
import functools

import jax
import jax.numpy as jnp
from jax import lax
from jax.experimental import pallas as pl
from jax.experimental.pallas import tpu as pltpu

D_MODEL = 1024
SSM_WIDTH = 512
SSM_CH = 16
SSM_GROUPS = 32
SSM_STATE = 64
GM_WIDTH = 512
GM_HEADS = 4
GM_HEAD_DIM = 128
GM_CHUNK = 128
D_FF = 2816
IN_COLS = SSM_WIDTH + 2 * GM_WIDTH
EPS = 1e-6

SSM_T = 128
SSM_K = SSM_CH * SSM_T
FF_CHUNK = 1408
VMEM_LIMIT_BYTES = 56 * 1024 * 1024

F32 = jnp.float32
BF16 = jnp.bfloat16


def _rms(x, g):
    return x * lax.rsqrt(jnp.mean(x * x, axis=-1, keepdims=True) + EPS) * g


def _dot(a, b):
    return jnp.dot(a, b, preferred_element_type=F32)


def _const_spec(shape):
    return pl.BlockSpec(shape, lambda *_: (0,) * len(shape), pipeline_mode=pl.Buffered(1))


def _params(n_axes=1):
    return pltpu.CompilerParams(dimension_semantics=("arbitrary",) * n_axes,
                                vmem_limit_bytes=VMEM_LIMIT_BYTES)


def _ffn_body(x_ref, g_ref, win_ref, wout_ref, gf_ref, o_ref, acc_ref, *, final_norm):
    x = x_ref[...]
    h = _rms(x, g_ref[...]).astype(BF16)
    for j in range(D_FF // FF_CHUNK):
        lo = j * FF_CHUNK
        gate = _dot(h, win_ref[:, lo:lo + FF_CHUNK])
        up = _dot(h, win_ref[:, D_FF + lo:D_FF + lo + FF_CHUNK])
        act = (gate * jax.nn.sigmoid(gate) * up).astype(BF16)
        part = _dot(act, wout_ref[lo:lo + FF_CHUNK, :])
        if j == 0:
            acc_ref[...] = part
        else:
            acc_ref[...] += part
    y = x + 0.5 * acc_ref[...]
    if final_norm:
        y = _rms(y, gf_ref[...])
    o_ref[...] = y


def _ffn(x, gain, w_in, w_out, gain_final, *, final_norm, tm):
    n = x.shape[0]
    row = lambda i: (i, 0)
    return pl.pallas_call(
        functools.partial(_ffn_body, final_norm=final_norm),
        grid=(n // tm,),
        in_specs=[pl.BlockSpec((tm, D_MODEL), row),
                  _const_spec((1, D_MODEL)),
                  _const_spec((D_MODEL, 2 * D_FF)),
                  _const_spec((D_FF, D_MODEL)),
                  _const_spec((1, D_MODEL))],
        out_specs=pl.BlockSpec((tm, D_MODEL), row),
        out_shape=jax.ShapeDtypeStruct((n, D_MODEL), F32),
        scratch_shapes=[pltpu.VMEM((tm, D_MODEL), F32)],
        compiler_params=_params(),
        name="ffn",
    )(x, gain, w_in, w_out, gain_final)


def _mixin_body(x_ref, g_ref, w_ref, vg_ref, ws_ref, bs_ref, go_ref, us_ref, yg_ref, y_scr, *, tm):
    h = _rms(x_ref[...], g_ref[...]).astype(BF16)
    z = _dot(h, w_ref[...])
    us_ref[...] = z[:, :SSM_WIDTH].astype(BF16)
    t_idx = lax.broadcasted_iota(jnp.int32, (GM_CHUNK, GM_CHUNK), 0)
    s_idx = lax.broadcasted_iota(jnp.int32, (GM_CHUNK, GM_CHUNK), 1)
    causal = s_idx <= t_idx
    for hd in range(GM_HEADS):
        lo = hd * GM_HEAD_DIM
        u = jax.nn.gelu(z[:, SSM_WIDTH + lo:SSM_WIDTH + lo + GM_HEAD_DIM])
        v = jax.nn.gelu(z[:, SSM_WIDTH + GM_WIDTH + lo:SSM_WIDTH + GM_WIDTH + lo + GM_HEAD_DIM])
        v = _rms(v, vg_ref[:, lo:lo + GM_HEAD_DIM]).astype(BF16)
        ws = jnp.where(causal, ws_ref[hd], 0.0).astype(BF16)
        for j in range(tm // GM_CHUNK):
            r0 = j * GM_CHUNK
            s = _dot(ws, v[r0:r0 + GM_CHUNK]) + bs_ref[hd]
            y_scr[r0:r0 + GM_CHUNK, lo:lo + GM_HEAD_DIM] = u[r0:r0 + GM_CHUNK] * s
    yg_ref[...] = _rms(y_scr[...], go_ref[...]).astype(BF16)


def _mixin(x, gain, w_in, v_gain, w_s, b_s, gain_gm_out, *, tm):
    n = x.shape[0]
    row = lambda i: (i, 0)
    return pl.pallas_call(
        functools.partial(_mixin_body, tm=tm),
        grid=(n // tm,),
        in_specs=[pl.BlockSpec((tm, D_MODEL), row),
                  _const_spec((1, D_MODEL)),
                  _const_spec((D_MODEL, IN_COLS)),
                  _const_spec((1, GM_WIDTH)),
                  _const_spec((GM_HEADS, GM_CHUNK, GM_CHUNK)),
                  _const_spec((GM_HEADS, GM_CHUNK, 1)),
                  _const_spec((1, GM_WIDTH))],
        out_specs=[pl.BlockSpec((tm, SSM_WIDTH), row), pl.BlockSpec((tm, GM_WIDTH), row)],
        out_shape=[jax.ShapeDtypeStruct((n, SSM_WIDTH), BF16),
                   jax.ShapeDtypeStruct((n, GM_WIDTH), BF16)],
        scratch_shapes=[pltpu.VMEM((tm, GM_WIDTH), F32)],
        compiler_params=_params(),
        name="mixin_gmlp",
    )(x, gain, w_in, v_gain, w_s, b_s, gain_gm_out)


def _cmul(ar, ai, br, bi):
    return ar * br - ai * bi, ar * bi + ai * br


def _ssm_ops_body(ldt_ref, arc_ref, aic_ref, arr_ref, air_ref, ar2_ref, ai2_ref, cx_re_ref, cx_im_ref,
                  bx_re_ref, bx_im_ref, ct_re_ref, ct_im_ref, b_re_ref, b_im_ref,
                  m_ref, et_ref, f_ref, lam_ref, kt_scr):
    dt = jnp.exp(ldt_ref[0])

    def zoh(a_re, a_im):
        xr, xi = a_re * dt, a_im * dt
        mag = jnp.exp(xr)
        nr, ni = mag * jnp.cos(xi) - 1.0, mag * jnp.sin(xi)
        den = a_re * a_re + a_im * a_im
        return xr, xi, (nr * a_re + ni * a_im) / den, (ni * a_re - nr * a_im) / den

    def powers(xr, xi, k):
        mag = jnp.exp(xr * k)
        return mag * jnp.cos(xi * k), mag * jnp.sin(xi * k)

    xr_c, xi_c, qr_c, qi_c = zoh(arc_ref[0], aic_ref[0])
    k_lane = lax.broadcasted_iota(jnp.int32, (SSM_STATE, SSM_T), 1).astype(F32)
    v_re, v_im = powers(xr_c, xi_c, k_lane)
    v1_re, v1_im = powers(xr_c, xi_c, k_lane + 1.0)
    vr_re, vr_im = powers(xr_c, xi_c, (SSM_T - 1.0) - k_lane)

    xr_r, xi_r, qr_r, qi_r = zoh(arr_ref[0], air_ref[0])
    bb_re, bb_im = _cmul(qr_r, qi_r, bx_re_ref[0], bx_im_ref[0])
    w_re, w_im = _cmul(cx_re_ref[0], cx_im_ref[0], bb_re, bb_im)
    exact = functools.partial(jnp.dot, preferred_element_type=F32, precision=lax.Precision.HIGHEST)
    kt_scr[...] = exact(w_re, v_re) - exact(w_im, v_im)

    s_idx = lax.broadcasted_iota(jnp.int32, (SSM_T, SSM_T), 0)
    t_idx = lax.broadcasted_iota(jnp.int32, (SSM_T, SSM_T), 1)
    causal = t_idx >= s_idx
    for c in range(SSM_CH):
        def per_in_channel(cp, carry, c=c):
            kv = kt_scr[pl.ds(c * SSM_CH + cp, 1), :]
            rows = jnp.broadcast_to(kv, (SSM_T, SSM_T))
            toep = pltpu.roll(rows, 0, 1, stride=1, stride_axis=0)
            r0 = pl.multiple_of(cp * SSM_T, SSM_T)
            m_ref[0, pl.ds(r0, SSM_T), c * SSM_T:(c + 1) * SSM_T] = (
                jnp.where(causal, toep, 0.0).astype(BF16))
            return carry
        lax.fori_loop(0, SSM_CH, per_in_channel, 0)

    bbc_re, bbc_im = _cmul(qr_c, qi_c, b_re_ref[0], b_im_ref[0])
    for c in range(SSM_CH):
        cols = slice(c * SSM_T, (c + 1) * SSM_T)
        e_re, e_im = _cmul(bbc_re[:, c:c + 1], bbc_im[:, c:c + 1], vr_re, vr_im)
        et_ref[0, 0:SSM_STATE, cols] = e_re.astype(BF16)
        et_ref[0, SSM_STATE:2 * SSM_STATE, cols] = e_im.astype(BF16)
        f_re, f_im = _cmul(ct_re_ref[0][:, c:c + 1], ct_im_ref[0][:, c:c + 1], v1_re, v1_im)
        f_ref[0, 0:SSM_STATE, cols] = f_re.astype(BF16)
        f_ref[0, SSM_STATE:2 * SSM_STATE, cols] = (-f_im).astype(BF16)

    xr_2, xi_2, _, _ = zoh(ar2_ref[0], ai2_ref[0])
    lt_re, lt_im = powers(xr_2, xi_2, float(SSM_T))
    im_half = lax.broadcasted_iota(jnp.int32, (1, 2 * SSM_STATE), 1) >= SSM_STATE
    lam_ref[0, 0:1, :] = lt_re
    lam_ref[0, 1:2, :] = jnp.where(im_half, lt_im, -lt_im)


def _ssm_ops(log_dt, a_re, a_im, b_re, b_im, c_re, c_im):
    g, p, c = SSM_GROUPS, SSM_STATE, SSM_CH
    swap = lambda a: jnp.swapaxes(a, 1, 2)
    args = [
        log_dt.reshape(g, 1, 1),
        a_re.reshape(g, p, 1), a_im.reshape(g, p, 1),
        a_re.reshape(g, 1, p), a_im.reshape(g, 1, p),
        jnp.tile(a_re, (1, 2)).reshape(g, 1, 2 * p), jnp.tile(a_im, (1, 2)).reshape(g, 1, 2 * p),
        jnp.repeat(c_re, c, axis=1), jnp.repeat(c_im, c, axis=1),
        jnp.tile(swap(b_re), (1, c, 1)), jnp.tile(swap(b_im), (1, c, 1)),
        swap(c_re), swap(c_im),
        b_re, b_im,
    ]
    spec = lambda a: pl.BlockSpec((1,) + a.shape[1:], lambda i: (i, 0, 0))
    out_shapes = [jax.ShapeDtypeStruct((g, SSM_K, SSM_K), BF16),
                  jax.ShapeDtypeStruct((g, 2 * p, SSM_K), BF16),
                  jax.ShapeDtypeStruct((g, 2 * p, SSM_K), BF16),
                  jax.ShapeDtypeStruct((g, 2, 2 * p), F32)]
    return pl.pallas_call(
        _ssm_ops_body,
        grid=(g,),
        in_specs=[spec(a) for a in args],
        out_specs=[spec(s) for s in out_shapes],
        out_shape=out_shapes,
        scratch_shapes=[pltpu.VMEM((c * c, SSM_T), F32)],
        compiler_params=_params(),
        name="ssm_operators",
    )(*args)


def _ssm_body(u_ref, m_ref, et_ref, f_ref, lam_ref, d_ref, y_ref, h_scr, *, n_chunks, bsz):
    u = u_ref[0]
    local_end = lax.dot_general(u, et_ref[0], (((1,), (1,)), ((), ())),
                                preferred_element_type=F32)
    lam_a, lam_b = lam_ref[0, 0:1, :], lam_ref[0, 1:2, :]
    h = jnp.zeros((bsz, 2 * SSM_STATE), F32)
    for n in range(n_chunks):
        h_scr[n * bsz:(n + 1) * bsz, :] = h
        h = h * lam_a + pltpu.roll(h, SSM_STATE, 1) * lam_b + local_end[n * bsz:(n + 1) * bsz]
    y = _dot(u, m_ref[0]) + _dot(h_scr[...].astype(BF16), f_ref[0]) + d_ref[0] * u.astype(F32)
    y_ref[0] = jax.nn.gelu(y).astype(BF16)


def _ssm(u, m, et, f, lam_t, d_exp, *, n_chunks, bsz):
    g, rows = u.shape[0], u.shape[1]
    spec = lambda a: pl.BlockSpec((1,) + a.shape[1:], lambda i: (i, 0, 0))
    return pl.pallas_call(
        functools.partial(_ssm_body, n_chunks=n_chunks, bsz=bsz),
        grid=(g,),
        in_specs=[spec(a) for a in (u, m, et, f, lam_t, d_exp)],
        out_specs=spec(u),
        out_shape=jax.ShapeDtypeStruct(u.shape, BF16),
        scratch_shapes=[pltpu.VMEM((rows, 2 * SSM_STATE), F32)],
        compiler_params=_params(),
        name="ssm_apply",
    )(u, m, et, f, lam_t, d_exp)


def _mixout_body(x_ref, a_ref, yg_ref, wg_ref, bg_ref, gs_ref, wo_ref, o_ref):
    z = _dot(a_ref[...], wg_ref[...]) + bg_ref[...]
    o = z[:, :SSM_WIDTH] * jax.nn.sigmoid(z[:, SSM_WIDTH:])
    o = _rms(o, gs_ref[...]).astype(BF16)
    o_ref[...] = (x_ref[...] + _dot(o, wo_ref[:SSM_WIDTH, :])
                  + _dot(yg_ref[...], wo_ref[SSM_WIDTH:, :]))


def _mixout(x, a_ssm, y_gm, w_glu, b_glu, gain_ssm_out, w_out, *, tm):
    n = x.shape[0]
    row = lambda i: (i, 0)
    return pl.pallas_call(
        _mixout_body,
        grid=(n // tm,),
        in_specs=[pl.BlockSpec((tm, D_MODEL), row),
                  pl.BlockSpec((tm, SSM_WIDTH), row),
                  pl.BlockSpec((tm, GM_WIDTH), row),
                  _const_spec((SSM_WIDTH, 2 * SSM_WIDTH)),
                  _const_spec((1, 2 * SSM_WIDTH)),
                  _const_spec((1, SSM_WIDTH)),
                  _const_spec((SSM_WIDTH + GM_WIDTH, D_MODEL))],
        out_specs=pl.BlockSpec((tm, D_MODEL), row),
        out_shape=jax.ShapeDtypeStruct((n, D_MODEL), F32),
        compiler_params=_params(),
        name="mixout",
    )(x, a_ssm, y_gm, w_glu, b_glu, gain_ssm_out, w_out)


def _glu_block_diag(glu_w, glu_b):
    g, c = SSM_GROUPS, SSM_CH
    w4 = glu_w.reshape(g, c, 2, c)
    same = jnp.eye(g, dtype=bool)[:, None, None, :, None]
    dense = jnp.where(same, w4[:, :, :, None, :], 0.0)
    bias = glu_b.reshape(g, 2, c).transpose(1, 0, 2).reshape(1, 2 * g * c)
    return dense.reshape(g * c, 2 * g * c).astype(BF16), bias


def _token_tile(n, want):
    tm = min(want, n)
    assert n % tm == 0 and tm % GM_CHUNK == 0
    return tm


def kernel(x, norm_ffn1, ffn1_w_in, ffn1_w_out, norm_mix, mix_w_in, ssm_a_re, ssm_a_im, ssm_log_dt, ssm_b_re, ssm_b_im, ssm_c_re, ssm_c_im, ssm_d, ssm_glu_w, ssm_glu_b, gm_v_gain, gm_w_s, gm_b_s, gain_ssm_out, gain_gm_out, mix_w_out, norm_ffn2, ffn2_w_in, ffn2_w_out, norm_final):
    bsz, seq, _ = x.shape
    depth = norm_ffn1.shape[0]
    n = bsz * seq
    n_chunks = seq // SSM_T
    assert seq % SSM_T == 0 and bsz % 8 == 0
    tm = _token_tile(n, 512)
    g, c = SSM_GROUPS, SSM_CH
    gain_final = norm_final.reshape(1, D_MODEL)

    h = x.reshape(n, D_MODEL)
    for l in range(depth):
        h = _ffn(h, norm_ffn1[l].reshape(1, D_MODEL), ffn1_w_in[l].astype(BF16),
                 ffn1_w_out[l].astype(BF16), gain_final, final_norm=False, tm=tm)

        u_ssm, y_gm = _mixin(h, norm_mix[l].reshape(1, D_MODEL), mix_w_in[l].astype(BF16),
                             gm_v_gain[l].reshape(1, GM_WIDTH), gm_w_s[l],
                             gm_b_s[l].reshape(GM_HEADS, GM_CHUNK, 1),
                             gain_gm_out[l].reshape(1, GM_WIDTH), tm=tm)

        m, et, f, lam_t = _ssm_ops(ssm_log_dt[l], ssm_a_re[l], ssm_a_im[l], ssm_b_re[l],
                                   ssm_b_im[l], ssm_c_re[l], ssm_c_im[l])
        u_rows = (u_ssm.reshape(bsz, n_chunks, SSM_T, g, c)
                  .transpose(3, 1, 0, 4, 2).reshape(g, n_chunks * bsz, SSM_K))
        d_exp = jnp.repeat(ssm_d[l], SSM_T, axis=1).reshape(g, 1, SSM_K)
        a_rows = _ssm(u_rows, m, et, f, lam_t, d_exp, n_chunks=n_chunks, bsz=bsz)
        a_ssm = (a_rows.reshape(g, n_chunks, bsz, c, SSM_T)
                 .transpose(2, 1, 4, 0, 3).reshape(n, SSM_WIDTH))

        w_glu, b_glu = _glu_block_diag(ssm_glu_w[l], ssm_glu_b[l])
        h = _mixout(h, a_ssm, y_gm, w_glu, b_glu, gain_ssm_out[l].reshape(1, SSM_WIDTH),
                    mix_w_out[l].astype(BF16), tm=tm)

        h = _ffn(h, norm_ffn2[l].reshape(1, D_MODEL), ffn2_w_in[l].astype(BF16),
                 ffn2_w_out[l].astype(BF16), gain_final, final_norm=(l == depth - 1), tm=tm)
    return h.reshape(bsz, seq, D_MODEL)
```

```python
import functools

import jax
import jax.numpy as jnp
from jax import lax
from jax.experimental import pallas as pl
from jax.experimental.pallas import tpu as pltpu

D_MODEL = 1024
SSM_WIDTH = 512
SSM_CH = 16
SSM_GROUPS = 32
SSM_STATE = 64
GM_WIDTH = 512
GM_HEADS = 4
GM_HEAD_DIM = 128
GM_CHUNK = 128
D_FF = 2816
IN_COLS = SSM_WIDTH + 2 * GM_WIDTH
EPS = 1e-6

SSM_T = 128
SSM_K = SSM_CH * SSM_T
FF_CHUNKS = (1024, 1024, 768)
assert sum(FF_CHUNKS) == D_FF
VMEM_LIMIT_BYTES = 56 * 1024 * 1024

F32 = jnp.float32
BF16 = jnp.bfloat16
I32 = jnp.int32


def _rms(x, g):
    return x * lax.rsqrt(jnp.mean(x * x, axis=-1, keepdims=True) + EPS) * g


def _dot(a, b):
    return jnp.dot(a, b, preferred_element_type=F32)


def _dot_nt(a, b):
    return lax.dot_general(a, b, (((1,), (1,)), ((), ())), preferred_element_type=F32)


def _dot_tn(a, b):
    return lax.dot_general(a, b, (((0,), (0,)), ((), ())), preferred_element_type=F32)


def _const_spec(shape):
    return pl.BlockSpec(shape, lambda *_: (0,) * len(shape), pipeline_mode=pl.Buffered(1))


def _params(n_axes=1):
    return pltpu.CompilerParams(dimension_semantics=("arbitrary",) * n_axes,
                                vmem_limit_bytes=VMEM_LIMIT_BYTES)


def _ffn_body(x_ref, g_ref, win_ref, wout_ref, gf_ref, o_ref, acc_ref, *, final_norm):
    x = x_ref[...]
    h = _rms(x, g_ref[...]).astype(BF16)
    lo = 0
    for width in FF_CHUNKS:
        gate = _dot(h, win_ref[:, lo:lo + width])
        up = _dot(h, win_ref[:, D_FF + lo:D_FF + lo + width])
        act = (gate * jax.nn.sigmoid(gate) * up).astype(BF16)
        part = _dot(act, wout_ref[lo:lo + width, :])
        if lo == 0:
            acc_ref[...] = part
        else:
            acc_ref[...] += part
        lo += width
    y = x + 0.5 * acc_ref[...]
    if final_norm:
        y = _rms(y, gf_ref[...])
    o_ref[...] = y


def _ffn(x, gain, w_in, w_out, gain_final, *, final_norm, tm):
    n = x.shape[0]
    row = lambda i: (i, 0)
    return pl.pallas_call(
        functools.partial(_ffn_body, final_norm=final_norm),
        grid=(n // tm,),
        in_specs=[pl.BlockSpec((tm, D_MODEL), row),
                  _const_spec((1, D_MODEL)),
                  _const_spec((D_MODEL, 2 * D_FF)),
                  _const_spec((D_FF, D_MODEL)),
                  _const_spec((1, D_MODEL))],
        out_specs=pl.BlockSpec((tm, D_MODEL), row),
        out_shape=jax.ShapeDtypeStruct((n, D_MODEL), F32),
        scratch_shapes=[pltpu.VMEM((tm, D_MODEL), F32)],
        compiler_params=_params(),
        name="ffn",
    )(x, gain, w_in, w_out, gain_final)


def _mixin_body(x_ref, g_ref, wu_ref, w_ref, vg_ref, ws_ref, bs_ref, go_ref, ut_ref, yg_ref, y_scr,
                *, tm):
    h = _rms(x_ref[...], g_ref[...]).astype(BF16)
    ut = _dot_nt(wu_ref[...], h)
    for j in range(tm // SSM_T):
        ut_ref[:, j, 0] = ut[:, j * SSM_T:(j + 1) * SSM_T].reshape(SSM_GROUPS, SSM_CH, SSM_T)
    z = _dot(h, w_ref[...])
    t_idx = lax.broadcasted_iota(I32, (GM_CHUNK, GM_CHUNK), 0)
    s_idx = lax.broadcasted_iota(I32, (GM_CHUNK, GM_CHUNK), 1)
    causal = s_idx <= t_idx
    for hd in range(GM_HEADS):
        lo = hd * GM_HEAD_DIM
        u = jax.nn.gelu(z[:, lo:lo + GM_HEAD_DIM])
        v = jax.nn.gelu(z[:, GM_WIDTH + lo:GM_WIDTH + lo + GM_HEAD_DIM])
        v = _rms(v, vg_ref[:, lo:lo + GM_HEAD_DIM]).astype(BF16)
        ws = jnp.where(causal, ws_ref[hd], 0.0).astype(BF16)
        for j in range(tm // GM_CHUNK):
            r0 = j * GM_CHUNK
            s = _dot(ws, v[r0:r0 + GM_CHUNK]) + bs_ref[hd]
            y_scr[r0:r0 + GM_CHUNK, lo:lo + GM_HEAD_DIM] = u[r0:r0 + GM_CHUNK] * s
    yg_ref[...] = _rms(y_scr[...], go_ref[...]).astype(BF16)


def _mixin(x, gain, wu_t, w_gm, v_gain, w_s, b_s, gain_gm_out, *, tm):
    bsz, seq, _ = x.shape
    k = tm // SSM_T
    tok = lambda b, i: (b, i, 0)
    return pl.pallas_call(
        functools.partial(_mixin_body, tm=tm),
        grid=(bsz, seq // tm),
        in_specs=[pl.BlockSpec((None, tm, D_MODEL), tok),
                  _const_spec((1, D_MODEL)),
                  _const_spec((SSM_WIDTH, D_MODEL)),
                  _const_spec((D_MODEL, 2 * GM_WIDTH)),
                  _const_spec((1, GM_WIDTH)),
                  _const_spec((GM_HEADS, GM_CHUNK, GM_CHUNK)),
                  _const_spec((GM_HEADS, GM_CHUNK, 1)),
                  _const_spec((1, GM_WIDTH))],
        out_specs=[pl.BlockSpec((SSM_GROUPS, k, 1, SSM_CH, SSM_T), lambda b, i: (0, i, b, 0, 0)),
                   pl.BlockSpec((None, tm, GM_WIDTH), tok)],
        out_shape=[jax.ShapeDtypeStruct((SSM_GROUPS, seq // SSM_T, bsz, SSM_CH, SSM_T), F32),
                   jax.ShapeDtypeStruct((bsz, seq, GM_WIDTH), BF16)],
        scratch_shapes=[pltpu.VMEM((tm, GM_WIDTH), F32)],
        compiler_params=_params(2),
        name="mixin_gmlp",
    )(x, gain, wu_t, w_gm, v_gain, w_s, b_s, gain_gm_out)


def _cmul(ar, ai, br, bi):
    return ar * br - ai * bi, ar * bi + ai * br


def _ssm_ops_body(ldt_ref, arc_ref, aic_ref, arr_ref, air_ref, ar2_ref, ai2_ref, cx_re_ref, cx_im_ref,
                  bx_re_ref, bx_im_ref, ct_re_ref, ct_im_ref, b_re_ref, b_im_ref,
                  m_ref, et_ref, f_ref, lam_ref):
    dt = jnp.exp(ldt_ref[0])

    def zoh(a_re, a_im):
        xr, xi = a_re * dt, a_im * dt
        mag = jnp.exp(xr)
        nr, ni = mag * jnp.cos(xi) - 1.0, mag * jnp.sin(xi)
        den = a_re * a_re + a_im * a_im
        return xr, xi, (nr * a_re + ni * a_im) / den, (ni * a_re - nr * a_im) / den

    def powers(xr, xi, k):
        mag = jnp.exp(xr * k)
        return mag * jnp.cos(xi * k), mag * jnp.sin(xi * k)

    xr_c, xi_c, qr_c, qi_c = zoh(arc_ref[0], aic_ref[0])
    k_lane = lax.broadcasted_iota(I32, (SSM_STATE, SSM_T), 1).astype(F32)
    v_re, v_im = powers(xr_c, xi_c, k_lane)
    v1_re, v1_im = powers(xr_c, xi_c, k_lane + 1.0)
    vr_re, vr_im = powers(xr_c, xi_c, (SSM_T - 1.0) - k_lane)

    xr_r, xi_r, qr_r, qi_r = zoh(arr_ref[0], air_ref[0])
    bb_re, bb_im = _cmul(qr_r, qi_r, bx_re_ref[0], bx_im_ref[0])
    w_re, w_im = _cmul(cx_re_ref[0], cx_im_ref[0], bb_re, bb_im)
    exact = functools.partial(jnp.dot, preferred_element_type=F32, precision=lax.Precision.HIGHEST)
    kt = exact(w_re, v_re) - exact(w_im, v_im)

    s_idx = lax.broadcasted_iota(I32, (SSM_T, SSM_T), 0)
    t_idx = lax.broadcasted_iota(I32, (SSM_T, SSM_T), 1)
    causal = t_idx >= s_idx
    for c in range(SSM_CH):
        for cp in range(SSM_CH):
            j = c * SSM_CH + cp
            taps = jnp.broadcast_to(kt[j:j + 1, :], (SSM_T, SSM_T))
            toep = pltpu.roll(taps, 0, 1, stride=1, stride_axis=0)
            m_ref[0, cp * SSM_T:(cp + 1) * SSM_T, c * SSM_T:(c + 1) * SSM_T] = (
                jnp.where(causal, toep, 0.0).astype(BF16))

    bbc_re, bbc_im = _cmul(qr_c, qi_c, b_re_ref[0], b_im_ref[0])
    for c in range(SSM_CH):
        cols = slice(c * SSM_T, (c + 1) * SSM_T)
        e_re, e_im = _cmul(bbc_re[:, c:c + 1], bbc_im[:, c:c + 1], vr_re, vr_im)
        et_ref[0, 0:SSM_STATE, cols] = e_re.astype(BF16)
        et_ref[0, SSM_STATE:2 * SSM_STATE, cols] = e_im.astype(BF16)
        f_re, f_im = _cmul(ct_re_ref[0][:, c:c + 1], ct_im_ref[0][:, c:c + 1], v1_re, v1_im)
        f_ref[0, 0:SSM_STATE, cols] = f_re.astype(BF16)
        f_ref[0, SSM_STATE:2 * SSM_STATE, cols] = (-f_im).astype(BF16)

    xr_2, xi_2, _, _ = zoh(ar2_ref[0], ai2_ref[0])
    lt_re, lt_im = powers(xr_2, xi_2, float(SSM_T))
    im_half = lax.broadcasted_iota(I32, (1, 2 * SSM_STATE), 1) >= SSM_STATE
    lam_ref[0, 0:1, :] = lt_re
    lam_ref[0, 1:2, :] = jnp.where(im_half, lt_im, -lt_im)


def _ssm_ops(log_dt, a_re, a_im, b_re, b_im, c_re, c_im):
    g, p, c = SSM_GROUPS, SSM_STATE, SSM_CH
    swap = lambda a: jnp.swapaxes(a, 1, 2)
    args = [
        log_dt.reshape(g, 1, 1),
        a_re.reshape(g, p, 1), a_im.reshape(g, p, 1),
        a_re.reshape(g, 1, p), a_im.reshape(g, 1, p),
        jnp.tile(a_re, (1, 2)).reshape(g, 1, 2 * p), jnp.tile(a_im, (1, 2)).reshape(g, 1, 2 * p),
        jnp.repeat(c_re, c, axis=1), jnp.repeat(c_im, c, axis=1),
        jnp.tile(swap(b_re), (1, c, 1)), jnp.tile(swap(b_im), (1, c, 1)),
        swap(c_re), swap(c_im),
        b_re, b_im,
    ]
    spec = lambda a: pl.BlockSpec((1,) + a.shape[1:], lambda i: (i, 0, 0))
    out_shapes = [jax.ShapeDtypeStruct((g, SSM_K, SSM_K), BF16),
                  jax.ShapeDtypeStruct((g, 2 * p, SSM_K), BF16),
                  jax.ShapeDtypeStruct((g, 2 * p, SSM_K), BF16),
                  jax.ShapeDtypeStruct((g, 2, 2 * p), F32)]
    return pl.pallas_call(
        _ssm_ops_body,
        grid=(g,),
        in_specs=[spec(a) for a in args],
        out_specs=[spec(s) for s in out_shapes],
        out_shape=out_shapes,
        compiler_params=_params(),
        name="ssm_operators",
    )(*args)


def _ssm_body(u_ref, m_ref, et_ref, f_ref, lam_ref, d_ref, y_ref, h_scr, *, n_chunks, bsz):
    rows = n_chunks * bsz
    uf = jnp.concatenate([u_ref[0, pl.ds(c, rows, stride=SSM_CH), :] for c in range(SSM_CH)],
                         axis=1)
    u = uf.astype(BF16)
    local_end = _dot_nt(u, et_ref[0])
    lam_a, lam_b = lam_ref[0, 0:1, :], lam_ref[0, 1:2, :]
    h = jnp.zeros((bsz, 2 * SSM_STATE), F32)
    for n in range(n_chunks):
        h_scr[n * bsz:(n + 1) * bsz, :] = h
        h = h * lam_a + pltpu.roll(h, SSM_STATE, 1) * lam_b + local_end[n * bsz:(n + 1) * bsz]
    y = _dot(u, m_ref[0]) + _dot(h_scr[...].astype(BF16), f_ref[0]) + d_ref[0] * uf
    a = jax.nn.gelu(y)
    for c in range(SSM_CH):
        y_ref[0, pl.ds(c, rows, stride=SSM_CH), :] = a[:, c * SSM_T:(c + 1) * SSM_T]


def _ssm(u, m, et, f, lam_t, d_exp, *, n_chunks, bsz):
    g = u.shape[0]
    spec = lambda a: pl.BlockSpec((1,) + a.shape[1:], lambda i: (i, 0, 0))
    return pl.pallas_call(
        functools.partial(_ssm_body, n_chunks=n_chunks, bsz=bsz),
        grid=(g,),
        in_specs=[spec(a) for a in (u, m, et, f, lam_t, d_exp)],
        out_specs=spec(u),
        out_shape=jax.ShapeDtypeStruct(u.shape, F32),
        scratch_shapes=[pltpu.VMEM((n_chunks * bsz, 2 * SSM_STATE), F32)],
        compiler_params=_params(),
        name="ssm_apply",
    )(u, m, et, f, lam_t, d_exp)


def _mixout_body(x_ref, at_ref, yg_ref, wg_ref, bg_ref, gs_ref, wo_ref, o_ref, *, tm):
    at = jnp.concatenate([at_ref[:, j, 0].reshape(SSM_WIDTH, SSM_T) for j in range(tm // SSM_T)],
                         axis=1).astype(BF16)
    zt = _dot(wg_ref[...], at) + bg_ref[...]
    ot = zt[:SSM_WIDTH] * jax.nn.sigmoid(zt[SSM_WIDTH:])
    ms = jnp.mean(ot * ot, axis=0, keepdims=True)
    ot = (ot * lax.rsqrt(ms + EPS) * gs_ref[...]).astype(BF16)
    o_ref[...] = (x_ref[...] + _dot_tn(ot, wo_ref[:SSM_WIDTH, :])
                  + _dot(yg_ref[...], wo_ref[SSM_WIDTH:, :]))


def _mixout(x, a_t, y_gm, w_glu_t, b_glu, gain_ssm_out, w_out, *, tm):
    bsz, seq, _ = x.shape
    k = tm // SSM_T
    tok = lambda b, i: (b, i, 0)
    return pl.pallas_call(
        functools.partial(_mixout_body, tm=tm),
        grid=(bsz, seq // tm),
        in_specs=[pl.BlockSpec((None, tm, D_MODEL), tok),
                  pl.BlockSpec((SSM_GROUPS, k, 1, SSM_CH, SSM_T), lambda b, i: (0, i, b, 0, 0)),
                  pl.BlockSpec((None, tm, GM_WIDTH), tok),
                  _const_spec((2 * SSM_WIDTH, SSM_WIDTH)),
                  _const_spec((2 * SSM_WIDTH, 1)),
                  _const_spec((SSM_WIDTH, 1)),
                  _const_spec((SSM_WIDTH + GM_WIDTH, D_MODEL))],
        out_specs=pl.BlockSpec((None, tm, D_MODEL), tok),
        out_shape=jax.ShapeDtypeStruct((bsz, seq, D_MODEL), F32),
        compiler_params=_params(2),
        name="mixout",
    )(x, a_t, y_gm, w_glu_t, b_glu, gain_ssm_out, w_out)


def _glu_block_diag_t(glu_w, glu_b):
    g, c = SSM_GROUPS, SSM_CH
    w4 = glu_w.reshape(g, c, 2, c).transpose(2, 0, 3, 1)
    same = jnp.eye(g, dtype=bool)[None, :, None, :, None]
    dense = jnp.where(same, w4[:, :, :, None, :], 0.0)
    bias = glu_b.reshape(g, 2, c).transpose(1, 0, 2).reshape(2 * g * c, 1)
    return dense.reshape(2 * g * c, g * c).astype(BF16), bias


def _tile(n, want, quantum):
    t = min(want, n)
    assert n % t == 0 and t % quantum == 0
    return t


def kernel(x, norm_ffn1, ffn1_w_in, ffn1_w_out, norm_mix, mix_w_in, ssm_a_re, ssm_a_im, ssm_log_dt, ssm_b_re, ssm_b_im, ssm_c_re, ssm_c_im, ssm_d, ssm_glu_w, ssm_glu_b, gm_v_gain, gm_w_s, gm_b_s, gain_ssm_out, gain_gm_out, mix_w_out, norm_ffn2, ffn2_w_in, ffn2_w_out, norm_final):
    bsz, seq, _ = x.shape
    depth = norm_ffn1.shape[0]
    n = bsz * seq
    n_chunks = seq // SSM_T
    assert seq % SSM_T == 0 and bsz % 8 == 0
    tm_ffn = _tile(n, 1024, 8)
    tm_mix = _tile(seq, 512, SSM_T)
    g, c = SSM_GROUPS, SSM_CH
    gain_final = norm_final.reshape(1, D_MODEL)
    flat = lambda a: a.reshape(n, D_MODEL)
    seqs = lambda a: a.reshape(bsz, seq, D_MODEL)

    h = flat(x)
    for l in range(depth):
        h = _ffn(h, norm_ffn1[l].reshape(1, D_MODEL), ffn1_w_in[l].astype(BF16),
                 ffn1_w_out[l].astype(BF16), gain_final, final_norm=False, tm=tm_ffn)

        w_in = mix_w_in[l].astype(BF16)
        u_t, y_gm = _mixin(seqs(h), norm_mix[l].reshape(1, D_MODEL), w_in[:, :SSM_WIDTH].T,
                           w_in[:, SSM_WIDTH:], gm_v_gain[l].reshape(1, GM_WIDTH), gm_w_s[l],
                           gm_b_s[l].reshape(GM_HEADS, GM_CHUNK, 1),
                           gain_gm_out[l].reshape(1, GM_WIDTH), tm=tm_mix)

        m, et, f, lam_t = _ssm_ops(ssm_log_dt[l], ssm_a_re[l], ssm_a_im[l], ssm_b_re[l],
                                   ssm_b_im[l], ssm_c_re[l], ssm_c_im[l])
        d_exp = jnp.repeat(ssm_d[l], SSM_T, axis=1).reshape(g, 1, SSM_K)
        a_t = _ssm(u_t.reshape(g, n_chunks * bsz * c, SSM_T), m, et, f, lam_t, d_exp,
                   n_chunks=n_chunks, bsz=bsz)

        w_glu_t, b_glu = _glu_block_diag_t(ssm_glu_w[l], ssm_glu_b[l])
        h = _mixout(seqs(h), a_t.reshape(u_t.shape), y_gm, w_glu_t, b_glu,
                    gain_ssm_out[l].reshape(SSM_WIDTH, 1), mix_w_out[l].astype(BF16), tm=tm_mix)

        h = _ffn(flat(h), norm_ffn2[l].reshape(1, D_MODEL), ffn2_w_in[l].astype(BF16),
                 ffn2_w_out[l].astype(BF16), gain_final, final_norm=(l == depth - 1), tm=tm_ffn)
    return seqs(h)
```

```python
import functools

import jax
import jax.numpy as jnp
from jax import lax
from jax.experimental import pallas as pl
from jax.experimental.pallas import tpu as pltpu

D_MODEL = 1024
SSM_WIDTH = 512
SSM_CH = 16
SSM_GROUPS = 32
SSM_STATE = 64
GM_WIDTH = 512
GM_HEADS = 4
GM_HEAD_DIM = 128
GM_CHUNK = 128
D_FF = 2816
IN_COLS = SSM_WIDTH + 2 * GM_WIDTH
EPS = 1e-6

SSM_T = 128
SSM_K = SSM_CH * SSM_T
MIX_CHUNKS = 8
MIX_TOKENS = MIX_CHUNKS * SSM_T
FF_CHUNKS = (1024, 1024, 768)
assert sum(FF_CHUNKS) == D_FF
VMEM_LIMIT_BYTES = 56 * 1024 * 1024

F32 = jnp.float32
BF16 = jnp.bfloat16
I32 = jnp.int32


def _rms(x, g):
    return x * lax.rsqrt(jnp.mean(x * x, axis=-1, keepdims=True) + EPS) * g


def _dot(a, b):
    return jnp.dot(a, b, preferred_element_type=F32)


def _dot_nt(a, b):
    return lax.dot_general(a, b, (((1,), (1,)), ((), ())), preferred_element_type=F32)


def _dot_tn(a, b):
    return lax.dot_general(a, b, (((0,), (0,)), ((), ())), preferred_element_type=F32)


def _const_spec(shape):
    return pl.BlockSpec(shape, lambda *_: (0,) * len(shape), pipeline_mode=pl.Buffered(1))


def _params(n_axes=1):
    return pltpu.CompilerParams(dimension_semantics=("arbitrary",) * n_axes,
                                vmem_limit_bytes=VMEM_LIMIT_BYTES)


def _ffn_body(x_ref, g_ref, win_ref, wout_ref, gf_ref, o_ref, acc_ref, *, final_norm):
    x = x_ref[...]
    h = _rms(x, g_ref[...]).astype(BF16)
    lo = 0
    for width in FF_CHUNKS:
        gate = _dot(h, win_ref[:, lo:lo + width])
        up = _dot(h, win_ref[:, D_FF + lo:D_FF + lo + width])
        act = (gate * jax.nn.sigmoid(gate) * up).astype(BF16)
        part = _dot(act, wout_ref[lo:lo + width, :])
        if lo == 0:
            acc_ref[...] = part
        else:
            acc_ref[...] += part
        lo += width
    y = x + 0.5 * acc_ref[...]
    if final_norm:
        y = _rms(y, gf_ref[...])
    o_ref[...] = y


def _ffn(x, gain, w_in, w_out, gain_final, *, final_norm, tm):
    n = x.shape[0]
    row = lambda i: (i, 0)
    return pl.pallas_call(
        functools.partial(_ffn_body, final_norm=final_norm),
        grid=(n // tm,),
        in_specs=[pl.BlockSpec((tm, D_MODEL), row),
                  _const_spec((1, D_MODEL)),
                  _const_spec((D_MODEL, 2 * D_FF)),
                  _const_spec((D_FF, D_MODEL)),
                  _const_spec((1, D_MODEL))],
        out_specs=pl.BlockSpec((tm, D_MODEL), row),
        out_shape=jax.ShapeDtypeStruct((n, D_MODEL), F32),
        scratch_shapes=[pltpu.VMEM((tm, D_MODEL), F32)],
        compiler_params=_params(),
        name="ffn",
    )(x, gain, w_in, w_out, gain_final)


def _mixin_body(x_ref, g_ref, wu_ref, w_ref, vg_ref, ws_ref, bs_ref, go_ref, ut_ref, yg_ref, y_scr,
                *, tm):
    h = _rms(x_ref[...], g_ref[...]).astype(BF16)
    ut = _dot_nt(wu_ref[...], h)
    for j in range(MIX_CHUNKS):
        ut_ref[pl.ds(j, SSM_WIDTH, stride=MIX_CHUNKS), :] = ut[:, j * SSM_T:(j + 1) * SSM_T]
    z = _dot(h, w_ref[...])
    t_idx = lax.broadcasted_iota(I32, (GM_CHUNK, GM_CHUNK), 0)
    s_idx = lax.broadcasted_iota(I32, (GM_CHUNK, GM_CHUNK), 1)
    causal = s_idx <= t_idx
    for hd in range(GM_HEADS):
        lo = hd * GM_HEAD_DIM
        u = jax.nn.gelu(z[:, lo:lo + GM_HEAD_DIM])
        v = jax.nn.gelu(z[:, GM_WIDTH + lo:GM_WIDTH + lo + GM_HEAD_DIM])
        v = _rms(v, vg_ref[:, lo:lo + GM_HEAD_DIM]).astype(BF16)
        ws = jnp.where(causal, ws_ref[hd], 0.0).astype(BF16)
        for j in range(tm // GM_CHUNK):
            r0 = j * GM_CHUNK
            s = _dot(ws, v[r0:r0 + GM_CHUNK]) + bs_ref[hd]
            y_scr[r0:r0 + GM_CHUNK, lo:lo + GM_HEAD_DIM] = u[r0:r0 + GM_CHUNK] * s
    yg_ref[...] = _rms(y_scr[...], go_ref[...]).astype(BF16)


def _mixin(x, gain, wu_t, w_gm, v_gain, w_s, b_s, gain_gm_out):
    bsz, seq, _ = x.shape
    tm = MIX_TOKENS
    tok = lambda b, i: (b, i, 0)
    act_spec = pl.BlockSpec((None, None, SSM_WIDTH * MIX_CHUNKS, SSM_T), lambda b, i: (b, i, 0, 0))
    return pl.pallas_call(
        functools.partial(_mixin_body, tm=tm),
        grid=(bsz, seq // tm),
        in_specs=[pl.BlockSpec((None, tm, D_MODEL), tok),
                  _const_spec((1, D_MODEL)),
                  _const_spec((SSM_WIDTH, D_MODEL)),
                  _const_spec((D_MODEL, 2 * GM_WIDTH)),
                  _const_spec((1, GM_WIDTH)),
                  _const_spec((GM_HEADS, GM_CHUNK, GM_CHUNK)),
                  _const_spec((GM_HEADS, GM_CHUNK, 1)),
                  _const_spec((1, GM_WIDTH))],
        out_specs=[act_spec, pl.BlockSpec((None, tm, GM_WIDTH), tok)],
        out_shape=[jax.ShapeDtypeStruct((bsz, seq // tm, SSM_WIDTH * MIX_CHUNKS, SSM_T), F32),
                   jax.ShapeDtypeStruct((bsz, seq, GM_WIDTH), BF16)],
        scratch_shapes=[pltpu.VMEM((tm, GM_WIDTH), F32)],
        compiler_params=_params(2),
        name="mixin_gmlp",
    )(x, gain, wu_t, w_gm, v_gain, w_s, b_s, gain_gm_out)


def _cmul(ar, ai, br, bi):
    return ar * br - ai * bi, ar * bi + ai * br


N_SSM_PARAMS = 15


def _ssm_operator_emitter(param_refs, kt_scr, lam_ref):
    (ldt_ref, arc_ref, aic_ref, arr_ref, air_ref, ar2_ref, ai2_ref, cx_re_ref, cx_im_ref,
     bx_re_ref, bx_im_ref, ct_re_ref, ct_im_ref, b_re_ref, b_im_ref) = param_refs
    dt = jnp.exp(ldt_ref[0])

    def zoh(a_re, a_im):
        xr, xi = a_re * dt, a_im * dt
        mag = jnp.exp(xr)
        nr, ni = mag * jnp.cos(xi) - 1.0, mag * jnp.sin(xi)
        den = a_re * a_re + a_im * a_im
        return xr, xi, (nr * a_re + ni * a_im) / den, (ni * a_re - nr * a_im) / den

    def powers(xr, xi, k):
        mag = jnp.exp(xr * k)
        return mag * jnp.cos(xi * k), mag * jnp.sin(xi * k)

    xr_c, xi_c, qr_c, qi_c = zoh(arc_ref[0], aic_ref[0])
    k_lane = lax.broadcasted_iota(I32, (SSM_STATE, SSM_T), 1).astype(F32)
    v_re, v_im = powers(xr_c, xi_c, k_lane)
    v1_re, v1_im = powers(xr_c, xi_c, k_lane + 1.0)
    vr_re, vr_im = powers(xr_c, xi_c, (SSM_T - 1.0) - k_lane)

    xr_r, xi_r, qr_r, qi_r = zoh(arr_ref[0], air_ref[0])
    bb_re, bb_im = _cmul(qr_r, qi_r, bx_re_ref[0], bx_im_ref[0])
    w_re, w_im = _cmul(cx_re_ref[0], cx_im_ref[0], bb_re, bb_im)
    exact = functools.partial(jnp.dot, preferred_element_type=F32, precision=lax.Precision.HIGHEST)
    kt_scr[...] = exact(w_re, v_re) - exact(w_im, v_im)

    xr_2, xi_2, _, _ = zoh(ar2_ref[0], ai2_ref[0])
    lt_re, lt_im = powers(xr_2, xi_2, float(SSM_T))
    im_half = lax.broadcasted_iota(I32, (1, 2 * SSM_STATE), 1) >= SSM_STATE
    lam_ref[0:1, :] = lt_re
    lam_ref[1:2, :] = jnp.where(im_half, lt_im, -lt_im)

    s_idx = lax.broadcasted_iota(I32, (SSM_T, SSM_T), 0)
    t_idx = lax.broadcasted_iota(I32, (SSM_T, SSM_T), 1)
    causal = t_idx >= s_idx
    bbc_re, bbc_im = _cmul(qr_c, qi_c, b_re_ref[0], b_im_ref[0])
    ct_re, ct_im = ct_re_ref[0], ct_im_ref[0]

    def emit(c, m_ref, et_ref, f_ref):
        cols = slice(c * SSM_T, (c + 1) * SSM_T)
        for cp in range(SSM_CH):
            j = c * SSM_CH + cp
            taps = jnp.broadcast_to(kt_scr[j:j + 1, :], (SSM_T, SSM_T))
            toep = pltpu.roll(taps, 0, 1, stride=1, stride_axis=0)
            m_ref[cp * SSM_T:(cp + 1) * SSM_T, cols] = jnp.where(causal, toep, 0.0).astype(BF16)
        e_re, e_im = _cmul(bbc_re[:, c:c + 1], bbc_im[:, c:c + 1], vr_re, vr_im)
        et_ref[0:SSM_STATE, cols] = e_re.astype(BF16)
        et_ref[SSM_STATE:2 * SSM_STATE, cols] = e_im.astype(BF16)
        f_re, f_im = _cmul(ct_re[:, c:c + 1], ct_im[:, c:c + 1], v1_re, v1_im)
        f_ref[0:SSM_STATE, cols] = f_re.astype(BF16)
        f_ref[SSM_STATE:2 * SSM_STATE, cols] = (-f_im).astype(BF16)

    return emit


def _ssm_param_views(log_dt, a_re, a_im, b_re, b_im, c_re, c_im):
    g, p, c = SSM_GROUPS, SSM_STATE, SSM_CH
    swap = lambda a: jnp.swapaxes(a, 1, 2)
    views = [
        log_dt.reshape(g, 1, 1),
        a_re.reshape(g, p, 1), a_im.reshape(g, p, 1),
        a_re.reshape(g, 1, p), a_im.reshape(g, 1, p),
        jnp.tile(a_re, (1, 2)).reshape(g, 1, 2 * p), jnp.tile(a_im, (1, 2)).reshape(g, 1, 2 * p),
        jnp.repeat(c_re, c, axis=1), jnp.repeat(c_im, c, axis=1),
        jnp.tile(swap(b_re), (1, c, 1)), jnp.tile(swap(b_im), (1, c, 1)),
        swap(c_re), swap(c_im),
        b_re, b_im,
    ]
    assert len(views) == N_SSM_PARAMS
    return views


SSM_N_TILE = 256


def _ssm_body(*refs, n_chunks, bsz):
    param_refs = refs[:N_SSM_PARAMS]
    u_ref, d_ref, y_ref = refs[N_SSM_PARAMS:N_SSM_PARAMS + 3]
    slots = refs[N_SSM_PARAMS + 3:-3]
    slot_a, slot_b = slots[:len(slots) // 2], slots[len(slots) // 2:]
    kt_scr, end_scr, h_scr = refs[-3:]
    rows = bsz * n_chunks
    slab_shape = (bsz, n_chunks // MIX_CHUNKS, MIX_CHUNKS, SSM_T)
    n_tiles = SSM_K // SSM_N_TILE
    step = pl.program_id(0)

    @pl.when(step == 0)
    def _zero_what_is_read_before_written():
        for ref in slot_b[:-1] + (slot_a[-1],):
            ref[...] = jnp.zeros(ref.shape, ref.dtype)

    def one_step(new, old):
        new_lhs, new_m, new_et, new_f, new_lam, new_raw = new
        old_lhs, old_m, old_et, old_f, old_lam, old_raw = old
        lam_a, lam_b = old_lam[0:1, :], old_lam[1:2, :]
        h = jnp.zeros((bsz, 2 * SSM_STATE), F32)

        end_scr[...] = _dot_nt(old_lhs[...], old_et[...])
        for j in range(n_tiles):
            cols = slice(j * SSM_N_TILE, (j + 1) * SSM_N_TILE)

            old_raw[:, cols] = _dot(old_lhs[...], old_m[:, cols])
            for n in range(j * n_chunks // n_tiles, (j + 1) * n_chunks // n_tiles):
                of_chunk = pl.ds(n, bsz, stride=n_chunks)
                h_scr[of_chunk, :] = h
                h = h * lam_a + pltpu.roll(h, SSM_STATE, 1) * lam_b + end_scr[of_chunk, :]

            if j == 0:
                emit = _ssm_operator_emitter(param_refs, kt_scr, new_lam)
            for c in range(j * SSM_CH // n_tiles, (j + 1) * SSM_CH // n_tiles):
                slab = slice(c * SSM_T, (c + 1) * SSM_T)
                of_channel = slice(c * MIX_CHUNKS, (c + 1) * MIX_CHUNKS)
                y_ref[:, :, of_channel, :] = jax.nn.gelu(new_raw[:, slab]).reshape(slab_shape)
                new_lhs[:, slab] = u_ref[:, :, of_channel, :].reshape(rows, SSM_T).astype(BF16)
                emit(c, new_m, new_et, new_f)

        h_in = h_scr[...].astype(BF16)
        for j in range(n_tiles):
            cols = slice(j * SSM_N_TILE, (j + 1) * SSM_N_TILE)
            old_raw[:, cols] += (_dot(h_in, old_f[:, cols])
                                 + d_ref[0, :, cols] * old_lhs[:, cols].astype(F32))

    parity = lax.rem(step, 2)
    pl.when(parity == 0)(lambda: one_step(slot_a, slot_b))
    pl.when(parity == 1)(lambda: one_step(slot_b, slot_a))


def _ssm(u, params, d_exp):
    bsz, n_mix_tiles = u.shape[:2]
    n_chunks = n_mix_tiles * MIX_CHUNKS
    g = SSM_GROUPS
    rows = n_chunks * bsz
    lag = lambda k: (lambda s: (jnp.clip(s - k, 0, g - 1), 0, 0))
    spec = lambda a, k: pl.BlockSpec((1,) + a.shape[1:], lag(k))
    act_spec = lambda k: pl.BlockSpec((bsz, n_mix_tiles, SSM_CH * MIX_CHUNKS, SSM_T),
                                      lambda s: (0, 0, jnp.clip(s - k, 0, g - 1), 0))
    slot = [pltpu.VMEM((rows, SSM_K), BF16),
            pltpu.VMEM((SSM_K, SSM_K), BF16),
            pltpu.VMEM((2 * SSM_STATE, SSM_K), BF16),
            pltpu.VMEM((2 * SSM_STATE, SSM_K), BF16),
            pltpu.VMEM((8, 2 * SSM_STATE), F32),
            pltpu.VMEM((rows, SSM_K), F32)]
    return pl.pallas_call(
        functools.partial(_ssm_body, n_chunks=n_chunks, bsz=bsz),
        grid=(g + 2,),
        in_specs=[spec(a, 0) for a in params] + [act_spec(0), spec(d_exp, 1)],
        out_specs=act_spec(2),
        out_shape=jax.ShapeDtypeStruct(u.shape, F32),
        scratch_shapes=slot + slot + [pltpu.VMEM((SSM_CH * SSM_CH, SSM_T), F32),
                                      pltpu.VMEM((rows, 2 * SSM_STATE), F32),
                                      pltpu.VMEM((rows, 2 * SSM_STATE), F32)],
        compiler_params=_params(),
        name="ssm",
    )(*params, u, d_exp)


def _mixout_body(x_ref, at_ref, yg_ref, wg_ref, bg_ref, gs_ref, wo_ref, o_ref):
    at = jnp.concatenate([at_ref[pl.ds(j, SSM_WIDTH, stride=MIX_CHUNKS), :]
                          for j in range(MIX_CHUNKS)], axis=1).astype(BF16)
    zt = _dot(wg_ref[...], at) + bg_ref[...]
    ot = zt[:SSM_WIDTH] * jax.nn.sigmoid(zt[SSM_WIDTH:])
    ms = jnp.mean(ot * ot, axis=0, keepdims=True)
    ot = (ot * lax.rsqrt(ms + EPS) * gs_ref[...]).astype(BF16)
    o_ref[...] = (x_ref[...] + _dot_tn(ot, wo_ref[:SSM_WIDTH, :])
                  + _dot(yg_ref[...], wo_ref[SSM_WIDTH:, :]))


def _mixout(x, a_t, y_gm, w_glu_t, b_glu, gain_ssm_out, w_out):
    bsz, seq, _ = x.shape
    tm = MIX_TOKENS
    tok = lambda b, i: (b, i, 0)
    return pl.pallas_call(
        _mixout_body,
        grid=(bsz, seq // tm),
        in_specs=[pl.BlockSpec((None, tm, D_MODEL), tok),
                  pl.BlockSpec((None, None, SSM_WIDTH * MIX_CHUNKS, SSM_T), lambda b, i: (b, i, 0, 0)),
                  pl.BlockSpec((None, tm, GM_WIDTH), tok),
                  _const_spec((2 * SSM_WIDTH, SSM_WIDTH)),
                  _const_spec((2 * SSM_WIDTH, 1)),
                  _const_spec((SSM_WIDTH, 1)),
                  _const_spec((SSM_WIDTH + GM_WIDTH, D_MODEL))],
        out_specs=pl.BlockSpec((None, tm, D_MODEL), tok),
        out_shape=jax.ShapeDtypeStruct((bsz, seq, D_MODEL), F32),
        compiler_params=_params(2),
        name="mixout",
    )(x, a_t, y_gm, w_glu_t, b_glu, gain_ssm_out, w_out)


def _glu_block_diag_t(glu_w, glu_b):
    g, c = SSM_GROUPS, SSM_CH
    w4 = glu_w.reshape(g, c, 2, c).transpose(2, 0, 3, 1)
    same = jnp.eye(g, dtype=bool)[None, :, None, :, None]
    dense = jnp.where(same, w4[:, :, :, None, :], 0.0)
    bias = glu_b.reshape(g, 2, c).transpose(1, 0, 2).reshape(2 * g * c, 1)
    return dense.reshape(2 * g * c, g * c).astype(BF16), bias


def _tile(n, want, quantum):
    t = min(want, n)
    assert n % t == 0 and t % quantum == 0
    return t


def kernel(x, norm_ffn1, ffn1_w_in, ffn1_w_out, norm_mix, mix_w_in, ssm_a_re, ssm_a_im, ssm_log_dt, ssm_b_re, ssm_b_im, ssm_c_re, ssm_c_im, ssm_d, ssm_glu_w, ssm_glu_b, gm_v_gain, gm_w_s, gm_b_s, gain_ssm_out, gain_gm_out, mix_w_out, norm_ffn2, ffn2_w_in, ffn2_w_out, norm_final):
    bsz, seq, _ = x.shape
    depth = norm_ffn1.shape[0]
    n = bsz * seq
    assert seq % MIX_TOKENS == 0 and bsz % 8 == 0
    tm_ffn = _tile(n, 1024, 8)
    g = SSM_GROUPS
    gain_final = norm_final.reshape(1, D_MODEL)
    flat = lambda a: a.reshape(n, D_MODEL)
    seqs = lambda a: a.reshape(bsz, seq, D_MODEL)

    h = flat(x)
    for l in range(depth):
        h = _ffn(h, norm_ffn1[l].reshape(1, D_MODEL), ffn1_w_in[l].astype(BF16),
                 ffn1_w_out[l].astype(BF16), gain_final, final_norm=False, tm=tm_ffn)

        w_in = mix_w_in[l].astype(BF16)
        u_t, y_gm = _mixin(seqs(h), norm_mix[l].reshape(1, D_MODEL), w_in[:, :SSM_WIDTH].T,
                           w_in[:, SSM_WIDTH:], gm_v_gain[l].reshape(1, GM_WIDTH), gm_w_s[l],
                           gm_b_s[l].reshape(GM_HEADS, GM_CHUNK, 1),
                           gain_gm_out[l].reshape(1, GM_WIDTH))

        ssm_params = _ssm_param_views(ssm_log_dt[l], ssm_a_re[l], ssm_a_im[l], ssm_b_re[l],
                                      ssm_b_im[l], ssm_c_re[l], ssm_c_im[l])
        d_exp = jnp.repeat(ssm_d[l], SSM_T, axis=1).reshape(g, 1, SSM_K)
        a_t = _ssm(u_t, ssm_params, d_exp)

        w_glu_t, b_glu = _glu_block_diag_t(ssm_glu_w[l], ssm_glu_b[l])
        h = _mixout(seqs(h), a_t, y_gm, w_glu_t, b_glu,
                    gain_ssm_out[l].reshape(SSM_WIDTH, 1), mix_w_out[l].astype(BF16))

        h = _ffn(flat(h), norm_ffn2[l].reshape(1, D_MODEL), ffn2_w_in[l].astype(BF16),
                 ffn2_w_out[l].astype(BF16), gain_final, final_norm=(l == depth - 1), tm=tm_ffn)
    return seqs(h)
```

```python
import functools

import jax
import jax.numpy as jnp
from jax import lax
from jax.experimental import pallas as pl
from jax.experimental.pallas import tpu as pltpu

D_MODEL = 1024
SSM_WIDTH = 512
SSM_CH = 16
SSM_GROUPS = 32
SSM_STATE = 64
GM_WIDTH = 512
GM_HEADS = 4
GM_HEAD_DIM = 128
GM_CHUNK = 128
D_FF = 2816
IN_COLS = SSM_WIDTH + 2 * GM_WIDTH
EPS = 1e-6

SSM_T = 128
SSM_K = SSM_CH * SSM_T
MIX_CHUNKS = 8
MIX_TOKENS = MIX_CHUNKS * SSM_T
FF_CHUNKS = (1024, 1024, 768)
assert sum(FF_CHUNKS) == D_FF
VMEM_LIMIT_BYTES = 56 * 1024 * 1024

F32 = jnp.float32
BF16 = jnp.bfloat16
I32 = jnp.int32


def _rms(x, g):
    return x * lax.rsqrt(jnp.mean(x * x, axis=-1, keepdims=True) + EPS) * g


def _dot(a, b):
    return jnp.dot(a, b, preferred_element_type=F32)


def _dot_nt(a, b):
    return lax.dot_general(a, b, (((1,), (1,)), ((), ())), preferred_element_type=F32)


def _dot_tn(a, b):
    return lax.dot_general(a, b, (((0,), (0,)), ((), ())), preferred_element_type=F32)


def _const_spec(shape):
    return pl.BlockSpec(shape, lambda *_: (0,) * len(shape), pipeline_mode=pl.Buffered(1))


def _layer_spec(stacked, layer):
    rest = stacked.shape[1:]
    return pl.BlockSpec((None,) + rest, lambda *_: (layer,) + (0,) * len(rest),
                        pipeline_mode=pl.Buffered(1))


def _params(n_axes=1):
    return pltpu.CompilerParams(dimension_semantics=("arbitrary",) * n_axes,
                                vmem_limit_bytes=VMEM_LIMIT_BYTES)


def _ffn_body(x_ref, g_ref, win_ref, wout_ref, gf_ref, o_ref, acc_ref, *, final_norm):
    x = x_ref[...]
    h = _rms(x, g_ref[...]).astype(BF16)
    lo = 0
    for width in FF_CHUNKS:
        gate = _dot(h, win_ref[:, lo:lo + width])
        up = _dot(h, win_ref[:, D_FF + lo:D_FF + lo + width])
        act = (gate * jax.nn.sigmoid(gate) * up).astype(BF16)
        part = _dot(act, wout_ref[lo:lo + width, :])
        if lo == 0:
            acc_ref[...] = part
        else:
            acc_ref[...] += part
        lo += width
    y = x + 0.5 * acc_ref[...]
    if final_norm:
        y = _rms(y, gf_ref[...])
    o_ref[...] = y


def _ffn(x, gain, w_in, w_out, gain_final, *, layer, final_norm, tm):
    n = x.shape[0]
    row = lambda i: (i, 0)
    return pl.pallas_call(
        functools.partial(_ffn_body, final_norm=final_norm),
        grid=(n // tm,),
        in_specs=[pl.BlockSpec((tm, D_MODEL), row),
                  _const_spec((1, D_MODEL)),
                  _layer_spec(w_in, layer),
                  _layer_spec(w_out, layer),
                  _const_spec((1, D_MODEL))],
        out_specs=pl.BlockSpec((tm, D_MODEL), row),
        out_shape=jax.ShapeDtypeStruct((n, D_MODEL), F32),
        scratch_shapes=[pltpu.VMEM((tm, D_MODEL), F32)],
        compiler_params=_params(),
        name="ffn",
    )(x, gain, w_in, w_out, gain_final)


def _mixin_body(x_ref, g_ref, wu_ref, w_ref, vg_ref, ws_ref, bs_ref, go_ref, ut_ref, yg_ref,
                h_scr, z_scr, y_scr, *, tm):
    pair = 2 * GM_CHUNK
    for r0 in range(0, tm, pair):
        h_scr[r0:r0 + pair, :] = _rms(x_ref[r0:r0 + pair, :], g_ref[...]).astype(BF16)
    z_scr[...] = _dot(h_scr[...], w_ref[:, SSM_WIDTH:])
    ut = _dot_nt(wu_ref[...], h_scr[...])
    for j in range(MIX_CHUNKS):
        ut_ref[pl.ds(j, SSM_WIDTH, stride=MIX_CHUNKS), :] = ut[:, j * SSM_T:(j + 1) * SSM_T]
    t_idx = lax.broadcasted_iota(I32, (GM_CHUNK, GM_CHUNK), 0)
    s_idx = lax.broadcasted_iota(I32, (GM_CHUNK, GM_CHUNK), 1)
    causal = s_idx <= t_idx
    for hd in range(GM_HEADS):
        lo = hd * GM_HEAD_DIM
        ws = jnp.where(causal, ws_ref[hd], 0.0).astype(BF16)
        for r0 in range(0, tm, pair):
            r1, r2 = r0 + GM_CHUNK, r0 + pair
            u = jax.nn.gelu(z_scr[r0:r2, lo:lo + GM_HEAD_DIM])
            v = jax.nn.gelu(z_scr[r0:r2, GM_WIDTH + lo:GM_WIDTH + lo + GM_HEAD_DIM])
            v = _rms(v, vg_ref[:, lo:lo + GM_HEAD_DIM]).astype(BF16)
            s = _dot(ws, jnp.concatenate([v[:GM_CHUNK], v[GM_CHUNK:]], axis=1)) + bs_ref[hd]
            y_scr[r0:r1, lo:lo + GM_HEAD_DIM] = u[:GM_CHUNK] * s[:, :GM_HEAD_DIM]
            y_scr[r1:r2, lo:lo + GM_HEAD_DIM] = u[GM_CHUNK:] * s[:, GM_HEAD_DIM:]
    for r0 in range(0, tm, pair):
        yg_ref[r0:r0 + pair, :] = _rms(y_scr[r0:r0 + pair, :], go_ref[...]).astype(BF16)


def _mixin(x, gain, wu_t, w_in, v_gain, w_s, b_s, gain_gm_out, *, layer):
    bsz, seq, _ = x.shape
    tm = MIX_TOKENS
    tok = lambda b, i: (b, i, 0)
    act_spec = pl.BlockSpec((None, None, SSM_WIDTH * MIX_CHUNKS, SSM_T), lambda b, i: (b, i, 0, 0))
    return pl.pallas_call(
        functools.partial(_mixin_body, tm=tm),
        grid=(bsz, seq // tm),
        in_specs=[pl.BlockSpec((None, tm, D_MODEL), tok),
                  _const_spec((1, D_MODEL)),
                  _layer_spec(wu_t, layer),
                  _layer_spec(w_in, layer),
                  _const_spec((1, GM_WIDTH)),
                  _const_spec((GM_HEADS, GM_CHUNK, GM_CHUNK)),
                  _const_spec((GM_HEADS, GM_CHUNK, 1)),
                  _const_spec((1, GM_WIDTH))],
        out_specs=[act_spec, pl.BlockSpec((None, tm, GM_WIDTH), tok)],
        out_shape=[jax.ShapeDtypeStruct((bsz, seq // tm, SSM_WIDTH * MIX_CHUNKS, SSM_T), F32),
                   jax.ShapeDtypeStruct((bsz, seq, GM_WIDTH), BF16)],
        scratch_shapes=[pltpu.VMEM((tm, D_MODEL), BF16),
                        pltpu.VMEM((tm, 2 * GM_WIDTH), F32),
                        pltpu.VMEM((tm, GM_WIDTH), F32)],
        compiler_params=_params(2),
        name="mixin_gmlp",
    )(x, gain, wu_t, w_in, v_gain, w_s, b_s, gain_gm_out)


def _cmul(ar, ai, br, bi):
    return ar * br - ai * bi, ar * bi + ai * br


N_SSM_PARAMS = 15


def _ssm_operator_emitter(param_refs, kt_scr, lam_ref):
    (ldt_ref, arc_ref, aic_ref, arr_ref, air_ref, ar2_ref, ai2_ref, cx_re_ref, cx_im_ref,
     bx_re_ref, bx_im_ref, ct_re_ref, ct_im_ref, b_re_ref, b_im_ref) = param_refs
    dt = jnp.exp(ldt_ref[0])

    def zoh(a_re, a_im):
        xr, xi = a_re * dt, a_im * dt
        mag = jnp.exp(xr)
        nr, ni = mag * jnp.cos(xi) - 1.0, mag * jnp.sin(xi)
        den = a_re * a_re + a_im * a_im
        return xr, xi, (nr * a_re + ni * a_im) / den, (ni * a_re - nr * a_im) / den

    def powers(xr, xi, k):
        mag = jnp.exp(xr * k)
        return mag * jnp.cos(xi * k), mag * jnp.sin(xi * k)

    xr_c, xi_c, qr_c, qi_c = zoh(arc_ref[0], aic_ref[0])
    k_lane = lax.broadcasted_iota(I32, (SSM_STATE, SSM_T), 1).astype(F32)
    v_re, v_im = powers(xr_c, xi_c, k_lane)
    v1_re, v1_im = powers(xr_c, xi_c, k_lane + 1.0)
    vr_re, vr_im = powers(xr_c, xi_c, (SSM_T - 1.0) - k_lane)

    xr_r, xi_r, qr_r, qi_r = zoh(arr_ref[0], air_ref[0])
    bb_re, bb_im = _cmul(qr_r, qi_r, bx_re_ref[0], bx_im_ref[0])
    w_re, w_im = _cmul(cx_re_ref[0], cx_im_ref[0], bb_re, bb_im)
    exact = functools.partial(jnp.dot, preferred_element_type=F32, precision=lax.Precision.HIGHEST)

    def taps():
        kt_scr[...] = exact(w_re, v_re) - exact(w_im, v_im)

    xr_2, xi_2, _, _ = zoh(ar2_ref[0], ai2_ref[0])
    lt_re, lt_im = powers(xr_2, xi_2, float(SSM_T))
    im_half = lax.broadcasted_iota(I32, (1, 2 * SSM_STATE), 1) >= SSM_STATE
    lam_ref[0:1, :] = lt_re
    lam_ref[1:2, :] = jnp.where(im_half, lt_im, -lt_im)

    s_idx = lax.broadcasted_iota(I32, (SSM_T, SSM_T), 0)
    t_idx = lax.broadcasted_iota(I32, (SSM_T, SSM_T), 1)
    causal = t_idx >= s_idx
    bbc_re, bbc_im = _cmul(qr_c, qi_c, b_re_ref[0], b_im_ref[0])
    ct_re, ct_im = ct_re_ref[0], ct_im_ref[0]

    def emit(c, m_ref, et_ref, f_ref):
        cols = slice(c * SSM_T, (c + 1) * SSM_T)
        for cp in range(SSM_CH):
            j = c * SSM_CH + cp
            taps = jnp.broadcast_to(kt_scr[j:j + 1, :], (SSM_T, SSM_T))
            toep = pltpu.roll(taps, 0, 1, stride=1, stride_axis=0)
            m_ref[cp * SSM_T:(cp + 1) * SSM_T, cols] = jnp.where(causal, toep, 0.0).astype(BF16)
        e_re, e_im = _cmul(bbc_re[:, c:c + 1], bbc_im[:, c:c + 1], vr_re, vr_im)
        et_ref[0:SSM_STATE, cols] = e_re.astype(BF16)
        et_ref[SSM_STATE:2 * SSM_STATE, cols] = e_im.astype(BF16)
        f_re, f_im = _cmul(ct_re[:, c:c + 1], ct_im[:, c:c + 1], v1_re, v1_im)
        f_ref[0:SSM_STATE, cols] = f_re.astype(BF16)
        f_ref[SSM_STATE:2 * SSM_STATE, cols] = (-f_im).astype(BF16)

    return taps, emit


def _ssm_param_views(log_dt, a_re, a_im, b_re, b_im, c_re, c_im):
    g, p, c = SSM_GROUPS, SSM_STATE, SSM_CH
    swap = lambda a: jnp.swapaxes(a, 1, 2)
    views = [
        log_dt.reshape(g, 1, 1),
        a_re.reshape(g, p, 1), a_im.reshape(g, p, 1),
        a_re.reshape(g, 1, p), a_im.reshape(g, 1, p),
        jnp.tile(a_re, (1, 2)).reshape(g, 1, 2 * p), jnp.tile(a_im, (1, 2)).reshape(g, 1, 2 * p),
        jnp.repeat(c_re, c, axis=1), jnp.repeat(c_im, c, axis=1),
        jnp.tile(swap(b_re), (1, c, 1)), jnp.tile(swap(b_im), (1, c, 1)),
        swap(c_re), swap(c_im),
        b_re, b_im,
    ]
    assert len(views) == N_SSM_PARAMS
    return views


SSM_N_TILE = 256


def _ssm_body(*refs, n_chunks, bsz):
    param_refs = refs[:N_SSM_PARAMS]
    u_ref, d_ref, y_ref = refs[N_SSM_PARAMS:N_SSM_PARAMS + 3]
    slots = refs[N_SSM_PARAMS + 3:-3]
    slot_a, slot_b = slots[:len(slots) // 2], slots[len(slots) // 2:]
    kt_scr, end_scr, h_scr = refs[-3:]
    rows = bsz * n_chunks
    slab_shape = (bsz, n_chunks // MIX_CHUNKS, MIX_CHUNKS, SSM_T)
    n_tiles = SSM_K // SSM_N_TILE
    step = pl.program_id(0)

    @pl.when(step == 0)
    def _zero_what_is_read_before_written():
        for ref in slot_b[:-1] + (slot_a[-1],):
            ref[...] = jnp.zeros(ref.shape, ref.dtype)

    def one_step(new, old):
        new_lhs, new_m, new_et, new_f, new_lam, new_raw = new
        old_lhs, old_m, old_et, old_f, old_lam, old_raw = old
        lam_a, lam_b = old_lam[0:1, :], old_lam[1:2, :]
        h = jnp.zeros((bsz, 2 * SSM_STATE), F32)
        taps, emit = _ssm_operator_emitter(param_refs, kt_scr, new_lam)

        end_scr[...] = _dot_nt(old_lhs[...], old_et[...])
        for j in range(n_tiles):
            cols = slice(j * SSM_N_TILE, (j + 1) * SSM_N_TILE)

            old_raw[:, cols] = _dot(old_lhs[...], old_m[:, cols])
            for n in range(j * n_chunks // n_tiles, (j + 1) * n_chunks // n_tiles):
                of_chunk = pl.ds(n, bsz, stride=n_chunks)
                h_scr[of_chunk, :] = h
                h = h * lam_a + pltpu.roll(h, SSM_STATE, 1) * lam_b + end_scr[of_chunk, :]

            if j == 0:
                taps()
            for c in range(j * SSM_CH // n_tiles, (j + 1) * SSM_CH // n_tiles):
                slab = slice(c * SSM_T, (c + 1) * SSM_T)
                of_channel = slice(c * MIX_CHUNKS, (c + 1) * MIX_CHUNKS)
                y_ref[:, :, of_channel, :] = jax.nn.gelu(new_raw[:, slab]).reshape(slab_shape)
                new_lhs[:, slab] = u_ref[:, :, of_channel, :].reshape(rows, SSM_T).astype(BF16)
                emit(c, new_m, new_et, new_f)

        h_in = h_scr[...].astype(BF16)
        for j in range(n_tiles):
            cols = slice(j * SSM_N_TILE, (j + 1) * SSM_N_TILE)
            old_raw[:, cols] += (_dot(h_in, old_f[:, cols])
                                 + d_ref[0, :, cols] * old_lhs[:, cols].astype(F32))

    parity = lax.rem(step, 2)
    pl.when(parity == 0)(lambda: one_step(slot_a, slot_b))
    pl.when(parity == 1)(lambda: one_step(slot_b, slot_a))


def _ssm(u, params, d_exp):
    bsz, n_mix_tiles = u.shape[:2]
    n_chunks = n_mix_tiles * MIX_CHUNKS
    g = SSM_GROUPS
    rows = n_chunks * bsz
    lag = lambda k: (lambda s: (jnp.clip(s - k, 0, g - 1), 0, 0))
    spec = lambda a, k: pl.BlockSpec((1,) + a.shape[1:], lag(k))
    act_spec = lambda k: pl.BlockSpec((bsz, n_mix_tiles, SSM_CH * MIX_CHUNKS, SSM_T),
                                      lambda s: (0, 0, jnp.clip(s - k, 0, g - 1), 0))
    slot = [pltpu.VMEM((rows, SSM_K), BF16),
            pltpu.VMEM((SSM_K, SSM_K), BF16),
            pltpu.VMEM((2 * SSM_STATE, SSM_K), BF16),
            pltpu.VMEM((2 * SSM_STATE, SSM_K), BF16),
            pltpu.VMEM((8, 2 * SSM_STATE), F32),
            pltpu.VMEM((rows, SSM_K), F32)]
    return pl.pallas_call(
        functools.partial(_ssm_body, n_chunks=n_chunks, bsz=bsz),
        grid=(g + 2,),
        in_specs=[spec(a, 0) for a in params] + [act_spec(0), spec(d_exp, 1)],
        out_specs=act_spec(2),
        out_shape=jax.ShapeDtypeStruct(u.shape, F32),
        scratch_shapes=slot + slot + [pltpu.VMEM((SSM_CH * SSM_CH, SSM_T), F32),
                                      pltpu.VMEM((rows, 2 * SSM_STATE), F32),
                                      pltpu.VMEM((rows, 2 * SSM_STATE), F32)],
        compiler_params=_params(),
        name="ssm",
    )(*params, u, d_exp)


GLU_TILE = 256
assert GLU_TILE % SSM_CH == 0 and SSM_WIDTH % GLU_TILE == 0


def _mixout_body(x_ref, at_ref, yg_ref, wg_ref, bg_ref, gs_ref, wo_ref, o_ref):
    at = jnp.concatenate([at_ref[pl.ds(j, SSM_WIDTH, stride=MIX_CHUNKS), :]
                          for j in range(MIX_CHUNKS)], axis=1).astype(BF16)

    def glu_half(r0):
        return jnp.concatenate(
            [_dot(wg_ref[r0 + q * GLU_TILE:r0 + (q + 1) * GLU_TILE, q * GLU_TILE:(q + 1) * GLU_TILE],
                  at[q * GLU_TILE:(q + 1) * GLU_TILE])
             for q in range(SSM_WIDTH // GLU_TILE)], axis=0) + bg_ref[r0:r0 + SSM_WIDTH]

    ot = glu_half(0) * jax.nn.sigmoid(glu_half(SSM_WIDTH))
    ms = jnp.mean(ot * ot, axis=0, keepdims=True)
    ot = (ot * lax.rsqrt(ms + EPS) * gs_ref[...]).astype(BF16)
    o_ref[...] = (x_ref[...] + _dot(yg_ref[...], wo_ref[SSM_WIDTH:, :])
                  + _dot_tn(ot, wo_ref[:SSM_WIDTH, :]))


def _mixout(x, a_t, y_gm, w_glu_t, b_glu, gain_ssm_out, w_out, *, layer):
    bsz, seq, _ = x.shape
    tm = MIX_TOKENS
    tok = lambda b, i: (b, i, 0)
    return pl.pallas_call(
        _mixout_body,
        grid=(bsz, seq // tm),
        in_specs=[pl.BlockSpec((None, tm, D_MODEL), tok),
                  pl.BlockSpec((None, None, SSM_WIDTH * MIX_CHUNKS, SSM_T), lambda b, i: (b, i, 0, 0)),
                  pl.BlockSpec((None, tm, GM_WIDTH), tok),
                  _const_spec((2 * SSM_WIDTH, SSM_WIDTH)),
                  _const_spec((2 * SSM_WIDTH, 1)),
                  _const_spec((SSM_WIDTH, 1)),
                  _layer_spec(w_out, layer)],
        out_specs=pl.BlockSpec((None, tm, D_MODEL), tok),
        out_shape=jax.ShapeDtypeStruct((bsz, seq, D_MODEL), F32),
        compiler_params=_params(2),
        name="mixout",
    )(x, a_t, y_gm, w_glu_t, b_glu, gain_ssm_out, w_out)


def _glu_block_diag_t(glu_w, glu_b):
    g, c = SSM_GROUPS, SSM_CH
    w4 = glu_w.reshape(g, c, 2, c).transpose(2, 0, 3, 1)
    same = jnp.eye(g, dtype=bool)[None, :, None, :, None]
    dense = jnp.where(same, w4[:, :, :, None, :], 0.0)
    bias = glu_b.reshape(g, 2, c).transpose(1, 0, 2).reshape(2 * g * c, 1)
    return dense.reshape(2 * g * c, g * c).astype(BF16), bias


def _tile(n, want, quantum):
    t = min(want, n)
    assert n % t == 0 and t % quantum == 0
    return t


def kernel(x, norm_ffn1, ffn1_w_in, ffn1_w_out, norm_mix, mix_w_in, ssm_a_re, ssm_a_im, ssm_log_dt, ssm_b_re, ssm_b_im, ssm_c_re, ssm_c_im, ssm_d, ssm_glu_w, ssm_glu_b, gm_v_gain, gm_w_s, gm_b_s, gain_ssm_out, gain_gm_out, mix_w_out, norm_ffn2, ffn2_w_in, ffn2_w_out, norm_final):
    bsz, seq, _ = x.shape
    depth = norm_ffn1.shape[0]
    n = bsz * seq
    assert seq % MIX_TOKENS == 0 and bsz % 8 == 0
    tm_ffn = _tile(n, 1024, 8)
    g = SSM_GROUPS
    gain_final = norm_final.reshape(1, D_MODEL)
    flat = lambda a: a.reshape(n, D_MODEL)
    seqs = lambda a: a.reshape(bsz, seq, D_MODEL)

    ffn1_in, ffn1_out = ffn1_w_in.astype(BF16), ffn1_w_out.astype(BF16)
    ffn2_in, ffn2_out = ffn2_w_in.astype(BF16), ffn2_w_out.astype(BF16)
    mix_in, mix_out = mix_w_in.astype(BF16), mix_w_out.astype(BF16)
    mix_in_ssm_t = jnp.swapaxes(mix_in[:, :, :SSM_WIDTH], 1, 2)

    h = flat(x)
    for l in range(depth):
        h = _ffn(h, norm_ffn1[l].reshape(1, D_MODEL), ffn1_in, ffn1_out, gain_final,
                 layer=l, final_norm=False, tm=tm_ffn)

        u_t, y_gm = _mixin(seqs(h), norm_mix[l].reshape(1, D_MODEL), mix_in_ssm_t, mix_in,
                           gm_v_gain[l].reshape(1, GM_WIDTH), gm_w_s[l],
                           gm_b_s[l].reshape(GM_HEADS, GM_CHUNK, 1),
                           gain_gm_out[l].reshape(1, GM_WIDTH), layer=l)

        ssm_params = _ssm_param_views(ssm_log_dt[l], ssm_a_re[l], ssm_a_im[l], ssm_b_re[l],
                                      ssm_b_im[l], ssm_c_re[l], ssm_c_im[l])
        d_exp = jnp.repeat(ssm_d[l], SSM_T, axis=1).reshape(g, 1, SSM_K)
        a_t = _ssm(u_t, ssm_params, d_exp)

        w_glu_t, b_glu = _glu_block_diag_t(ssm_glu_w[l], ssm_glu_b[l])
        h = _mixout(seqs(h), a_t, y_gm, w_glu_t, b_glu,
                    gain_ssm_out[l].reshape(SSM_WIDTH, 1), mix_out, layer=l)

        h = _ffn(flat(h), norm_ffn2[l].reshape(1, D_MODEL), ffn2_in, ffn2_out, gain_final,
                 layer=l, final_norm=(l == depth - 1), tm=tm_ffn)
    return seqs(h)
```

```python
import functools

import jax
import jax.numpy as jnp
from jax import lax
from jax.experimental import pallas as pl
from jax.experimental.pallas import tpu as pltpu

D_MODEL = 1024
SSM_WIDTH = 512
SSM_CH = 16
SSM_GROUPS = 32
SSM_STATE = 64
GM_WIDTH = 512
GM_HEADS = 4
GM_HEAD_DIM = 128
GM_CHUNK = 128
D_FF = 2816
IN_COLS = SSM_WIDTH + 2 * GM_WIDTH
EPS = 1e-6

SSM_T = 128
SSM_K = SSM_CH * SSM_T
MIX_CHUNKS = 8
MIX_TOKENS = MIX_CHUNKS * SSM_T
FF_CHUNKS = (1024, 1024, 768)
assert sum(FF_CHUNKS) == D_FF
VMEM_LIMIT_BYTES = 56 * 1024 * 1024

F32 = jnp.float32
BF16 = jnp.bfloat16
I32 = jnp.int32


def _rms(x, g):
    return x * lax.rsqrt(jnp.mean(x * x, axis=-1, keepdims=True) + EPS) * g


def _dot(a, b):
    return jnp.dot(a, b, preferred_element_type=F32)


def _dot_nt(a, b):
    return lax.dot_general(a, b, (((1,), (1,)), ((), ())), preferred_element_type=F32)


def _dot_tn(a, b):
    return lax.dot_general(a, b, (((0,), (0,)), ((), ())), preferred_element_type=F32)


def _const_spec(shape):
    return pl.BlockSpec(shape, lambda *_: (0,) * len(shape), pipeline_mode=pl.Buffered(1))


def _layer_spec(stacked, layer):
    rest = stacked.shape[1:]
    return pl.BlockSpec((None,) + rest, lambda *_: (layer,) + (0,) * len(rest),
                        pipeline_mode=pl.Buffered(1))


def _params(n_axes=1):
    return pltpu.CompilerParams(dimension_semantics=("arbitrary",) * n_axes,
                                vmem_limit_bytes=VMEM_LIMIT_BYTES)


def _ffn_body(*refs, has_delta, final_norm):
    if has_delta:
        x_ref, d_ref, g_ref, win_ref, wout_ref, gf_ref, o_ref, acc_ref = refs
        x = x_ref[...] + d_ref[...]
    else:
        x_ref, g_ref, win_ref, wout_ref, gf_ref, o_ref, acc_ref = refs
        x = x_ref[...]
    h = _rms(x, g_ref[...]).astype(BF16)
    lo = 0
    for width in FF_CHUNKS:
        gate = _dot(h, win_ref[:, lo:lo + width])
        up = _dot(h, win_ref[:, D_FF + lo:D_FF + lo + width])
        act = (gate * jax.nn.sigmoid(gate) * up).astype(BF16)
        part = _dot(act, wout_ref[lo:lo + width, :])
        if lo == 0:
            acc_ref[...] = part
        else:
            acc_ref[...] += part
        lo += width
    y = x + 0.5 * acc_ref[...]
    if final_norm:
        y = _rms(y, gf_ref[...])
    o_ref[...] = y


def _ffn(x, delta, gain, w_in, w_out, gain_final, *, layer, final_norm, tm):
    n = x.shape[0]
    row = lambda i: (i, 0)
    streams = [x] if delta is None else [x, delta]
    return pl.pallas_call(
        functools.partial(_ffn_body, has_delta=delta is not None, final_norm=final_norm),
        grid=(n // tm,),
        in_specs=[pl.BlockSpec((tm, D_MODEL), row) for _ in streams] + [
                  _const_spec((1, D_MODEL)),
                  _layer_spec(w_in, layer),
                  _layer_spec(w_out, layer),
                  _const_spec((1, D_MODEL))],
        out_specs=pl.BlockSpec((tm, D_MODEL), row),
        out_shape=jax.ShapeDtypeStruct((n, D_MODEL), F32),
        scratch_shapes=[pltpu.VMEM((tm, D_MODEL), F32)],
        compiler_params=_params(),
        name="ffn",
    )(*streams, gain, w_in, w_out, gain_final)


def _mixin_body(x_ref, g_ref, wu_ref, w_ref, vg_ref, ws_ref, bs_ref, go_ref, ut_ref, yg_ref,
                h_scr, z_scr, y_scr, *, tm):
    pair = 2 * GM_CHUNK
    for r0 in range(0, tm, pair):
        h_scr[r0:r0 + pair, :] = _rms(x_ref[r0:r0 + pair, :], g_ref[...]).astype(BF16)
    z_scr[...] = _dot(h_scr[...], w_ref[:, SSM_WIDTH:])
    ut = _dot_nt(wu_ref[...], h_scr[...])
    for j in range(MIX_CHUNKS):
        ut_ref[pl.ds(j, SSM_WIDTH, stride=MIX_CHUNKS), :] = ut[:, j * SSM_T:(j + 1) * SSM_T]
    t_idx = lax.broadcasted_iota(I32, (GM_CHUNK, GM_CHUNK), 0)
    s_idx = lax.broadcasted_iota(I32, (GM_CHUNK, GM_CHUNK), 1)
    causal = s_idx <= t_idx
    for hd in range(GM_HEADS):
        lo = hd * GM_HEAD_DIM
        ws = jnp.where(causal, ws_ref[hd], 0.0).astype(BF16)
        for r0 in range(0, tm, pair):
            r1, r2 = r0 + GM_CHUNK, r0 + pair
            u = jax.nn.gelu(z_scr[r0:r2, lo:lo + GM_HEAD_DIM])
            v = jax.nn.gelu(z_scr[r0:r2, GM_WIDTH + lo:GM_WIDTH + lo + GM_HEAD_DIM])
            v = _rms(v, vg_ref[:, lo:lo + GM_HEAD_DIM]).astype(BF16)
            s = _dot(ws, jnp.concatenate([v[:GM_CHUNK], v[GM_CHUNK:]], axis=1)) + bs_ref[hd]
            y_scr[r0:r1, lo:lo + GM_HEAD_DIM] = u[:GM_CHUNK] * s[:, :GM_HEAD_DIM]
            y_scr[r1:r2, lo:lo + GM_HEAD_DIM] = u[GM_CHUNK:] * s[:, GM_HEAD_DIM:]
    for r0 in range(0, tm, pair):
        yg_ref[r0:r0 + pair, :] = _rms(y_scr[r0:r0 + pair, :], go_ref[...]).astype(BF16)


def _mixin(x, gain, wu_t, w_in, v_gain, w_s, b_s, gain_gm_out, *, layer):
    bsz, seq, _ = x.shape
    tm = MIX_TOKENS
    tok = lambda b, i: (b, i, 0)
    act_spec = pl.BlockSpec((None, None, SSM_WIDTH * MIX_CHUNKS, SSM_T), lambda b, i: (b, i, 0, 0))
    return pl.pallas_call(
        functools.partial(_mixin_body, tm=tm),
        grid=(bsz, seq // tm),
        in_specs=[pl.BlockSpec((None, tm, D_MODEL), tok),
                  _const_spec((1, D_MODEL)),
                  _layer_spec(wu_t, layer),
                  _layer_spec(w_in, layer),
                  _const_spec((1, GM_WIDTH)),
                  _const_spec((GM_HEADS, GM_CHUNK, GM_CHUNK)),
                  _const_spec((GM_HEADS, GM_CHUNK, 1)),
                  _const_spec((1, GM_WIDTH))],
        out_specs=[act_spec, pl.BlockSpec((None, tm, GM_WIDTH), tok)],
        out_shape=[jax.ShapeDtypeStruct((bsz, seq // tm, SSM_WIDTH * MIX_CHUNKS, SSM_T), F32),
                   jax.ShapeDtypeStruct((bsz, seq, GM_WIDTH), BF16)],
        scratch_shapes=[pltpu.VMEM((tm, D_MODEL), BF16),
                        pltpu.VMEM((tm, 2 * GM_WIDTH), F32),
                        pltpu.VMEM((tm, GM_WIDTH), F32)],
        compiler_params=_params(2),
        name="mixin_gmlp",
    )(x, gain, wu_t, w_in, v_gain, w_s, b_s, gain_gm_out)


def _cmul(ar, ai, br, bi):
    return ar * br - ai * bi, ar * bi + ai * br


N_SSM_PARAMS = 15


def _ssm_operator_emitter(param_refs, kt_scr, lam_ref):
    (ldt_ref, arc_ref, aic_ref, arr_ref, air_ref, ar2_ref, ai2_ref, cx_re_ref, cx_im_ref,
     bx_re_ref, bx_im_ref, ct_re_ref, ct_im_ref, b_re_ref, b_im_ref) = param_refs
    dt = jnp.exp(ldt_ref[0])

    def zoh(a_re, a_im):
        xr, xi = a_re * dt, a_im * dt
        mag = jnp.exp(xr)
        nr, ni = mag * jnp.cos(xi) - 1.0, mag * jnp.sin(xi)
        den = a_re * a_re + a_im * a_im
        return xr, xi, (nr * a_re + ni * a_im) / den, (ni * a_re - nr * a_im) / den

    def powers(xr, xi, k):
        mag = jnp.exp(xr * k)
        return mag * jnp.cos(xi * k), mag * jnp.sin(xi * k)

    xr_c, xi_c, qr_c, qi_c = zoh(arc_ref[0], aic_ref[0])
    k_lane = lax.broadcasted_iota(I32, (SSM_STATE, SSM_T), 1).astype(F32)
    v_re, v_im = powers(xr_c, xi_c, k_lane)
    v1_re, v1_im = powers(xr_c, xi_c, k_lane + 1.0)
    vr_re, vr_im = powers(xr_c, xi_c, (SSM_T - 1.0) - k_lane)

    xr_r, xi_r, qr_r, qi_r = zoh(arr_ref[0], air_ref[0])
    bb_re, bb_im = _cmul(qr_r, qi_r, bx_re_ref[0], bx_im_ref[0])
    w_re, w_im = _cmul(cx_re_ref[0], cx_im_ref[0], bb_re, bb_im)
    exact = functools.partial(jnp.dot, preferred_element_type=F32, precision=lax.Precision.HIGHEST)

    def taps():
        kt_scr[...] = exact(w_re, v_re) - exact(w_im, v_im)

    xr_2, xi_2, _, _ = zoh(ar2_ref[0], ai2_ref[0])
    lt_re, lt_im = powers(xr_2, xi_2, float(SSM_T))
    im_half = lax.broadcasted_iota(I32, (1, 2 * SSM_STATE), 1) >= SSM_STATE
    lam_ref[0:1, :] = lt_re
    lam_ref[1:2, :] = jnp.where(im_half, lt_im, -lt_im)

    s_idx = lax.broadcasted_iota(I32, (SSM_T, SSM_T), 0)
    t_idx = lax.broadcasted_iota(I32, (SSM_T, SSM_T), 1)
    causal = t_idx >= s_idx
    bbc_re, bbc_im = _cmul(qr_c, qi_c, b_re_ref[0], b_im_ref[0])
    ct_re, ct_im = ct_re_ref[0], ct_im_ref[0]

    def emit(c, m_ref, et_ref, f_ref):
        cols = slice(c * SSM_T, (c + 1) * SSM_T)
        for cp in range(SSM_CH):
            j = c * SSM_CH + cp
            taps = jnp.broadcast_to(kt_scr[j:j + 1, :], (SSM_T, SSM_T))
            toep = pltpu.roll(taps, 0, 1, stride=1, stride_axis=0)
            m_ref[cp * SSM_T:(cp + 1) * SSM_T, cols] = jnp.where(causal, toep, 0.0).astype(BF16)
        e_re, e_im = _cmul(bbc_re[:, c:c + 1], bbc_im[:, c:c + 1], vr_re, vr_im)
        et_ref[0:SSM_STATE, cols] = e_re.astype(BF16)
        et_ref[SSM_STATE:2 * SSM_STATE, cols] = e_im.astype(BF16)
        f_re, f_im = _cmul(ct_re[:, c:c + 1], ct_im[:, c:c + 1], v1_re, v1_im)
        f_ref[0:SSM_STATE, cols] = f_re.astype(BF16)
        f_ref[SSM_STATE:2 * SSM_STATE, cols] = (-f_im).astype(BF16)

    return taps, emit


def _ssm_param_views(log_dt, a_re, a_im, b_re, b_im, c_re, c_im):
    g, p, c = SSM_GROUPS, SSM_STATE, SSM_CH
    swap = lambda a: jnp.swapaxes(a, 1, 2)
    views = [
        log_dt.reshape(g, 1, 1),
        a_re.reshape(g, p, 1), a_im.reshape(g, p, 1),
        a_re.reshape(g, 1, p), a_im.reshape(g, 1, p),
        jnp.tile(a_re, (1, 2)).reshape(g, 1, 2 * p), jnp.tile(a_im, (1, 2)).reshape(g, 1, 2 * p),
        jnp.repeat(c_re, c, axis=1), jnp.repeat(c_im, c, axis=1),
        jnp.tile(swap(b_re), (1, c, 1)), jnp.tile(swap(b_im), (1, c, 1)),
        swap(c_re), swap(c_im),
        b_re, b_im,
    ]
    assert len(views) == N_SSM_PARAMS
    return views


SSM_N_TILE = 256


def _ssm_body(*refs, n_chunks, bsz):
    param_refs = refs[:N_SSM_PARAMS]
    u_ref, d_ref, y_ref = refs[N_SSM_PARAMS:N_SSM_PARAMS + 3]
    slots = refs[N_SSM_PARAMS + 3:-3]
    slot_a, slot_b = slots[:len(slots) // 2], slots[len(slots) // 2:]
    kt_scr, end_scr, h_scr = refs[-3:]
    rows = bsz * n_chunks
    slab_shape = (bsz, n_chunks // MIX_CHUNKS, MIX_CHUNKS, SSM_T)
    n_tiles = SSM_K // SSM_N_TILE
    step = pl.program_id(0)

    @pl.when(step == 0)
    def _zero_what_is_read_before_written():
        for ref in slot_b[:-1] + (slot_a[-1],):
            ref[...] = jnp.zeros(ref.shape, ref.dtype)

    def one_step(new, old):
        new_lhs, new_m, new_et, new_f, new_lam, new_raw = new
        old_lhs, old_m, old_et, old_f, old_lam, old_raw = old
        lam_a, lam_b = old_lam[0:1, :], old_lam[1:2, :]
        h = jnp.zeros((bsz, 2 * SSM_STATE), F32)
        taps, emit = _ssm_operator_emitter(param_refs, kt_scr, new_lam)

        end_scr[...] = _dot_nt(old_lhs[...], old_et[...])
        for j in range(n_tiles):
            cols = slice(j * SSM_N_TILE, (j + 1) * SSM_N_TILE)

            old_raw[:, cols] = _dot(old_lhs[...], old_m[:, cols])
            for n in range(j * n_chunks // n_tiles, (j + 1) * n_chunks // n_tiles):
                of_chunk = pl.ds(n, bsz, stride=n_chunks)
                h_scr[of_chunk, :] = h
                h = h * lam_a + pltpu.roll(h, SSM_STATE, 1) * lam_b + end_scr[of_chunk, :]

            if j == 0:
                taps()
            for c in range(j * SSM_CH // n_tiles, (j + 1) * SSM_CH // n_tiles):
                slab = slice(c * SSM_T, (c + 1) * SSM_T)
                of_channel = slice(c * MIX_CHUNKS, (c + 1) * MIX_CHUNKS)
                y_ref[:, :, of_channel, :] = jax.nn.gelu(new_raw[:, slab]).reshape(slab_shape)
                new_lhs[:, slab] = u_ref[:, :, of_channel, :].reshape(rows, SSM_T).astype(BF16)
                emit(c, new_m, new_et, new_f)

        h_in = h_scr[...].astype(BF16)
        for j in range(n_tiles):
            cols = slice(j * SSM_N_TILE, (j + 1) * SSM_N_TILE)
            old_raw[:, cols] += (_dot(h_in, old_f[:, cols])
                                 + d_ref[0, :, cols] * old_lhs[:, cols].astype(F32))

    parity = lax.rem(step, 2)
    pl.when(parity == 0)(lambda: one_step(slot_a, slot_b))
    pl.when(parity == 1)(lambda: one_step(slot_b, slot_a))


def _ssm(u, params, d_exp):
    bsz, n_mix_tiles = u.shape[:2]
    n_chunks = n_mix_tiles * MIX_CHUNKS
    g = SSM_GROUPS
    rows = n_chunks * bsz
    lag = lambda k: (lambda s: (jnp.clip(s - k, 0, g - 1), 0, 0))
    spec = lambda a, k: pl.BlockSpec((1,) + a.shape[1:], lag(k))
    act_spec = lambda k: pl.BlockSpec((bsz, n_mix_tiles, SSM_CH * MIX_CHUNKS, SSM_T),
                                      lambda s: (0, 0, jnp.clip(s - k, 0, g - 1), 0))
    slot = [pltpu.VMEM((rows, SSM_K), BF16),
            pltpu.VMEM((SSM_K, SSM_K), BF16),
            pltpu.VMEM((2 * SSM_STATE, SSM_K), BF16),
            pltpu.VMEM((2 * SSM_STATE, SSM_K), BF16),
            pltpu.VMEM((8, 2 * SSM_STATE), F32),
            pltpu.VMEM((rows, SSM_K), F32)]
    return pl.pallas_call(
        functools.partial(_ssm_body, n_chunks=n_chunks, bsz=bsz),
        grid=(g + 2,),
        in_specs=[spec(a, 0) for a in params] + [act_spec(0), spec(d_exp, 1)],
        out_specs=act_spec(2),
        out_shape=jax.ShapeDtypeStruct(u.shape, F32),
        scratch_shapes=slot + slot + [pltpu.VMEM((SSM_CH * SSM_CH, SSM_T), F32),
                                      pltpu.VMEM((rows, 2 * SSM_STATE), F32),
                                      pltpu.VMEM((rows, 2 * SSM_STATE), F32)],
        compiler_params=_params(),
        name="ssm",
    )(*params, u, d_exp)


GLU_TILE = 256
assert GLU_TILE % SSM_CH == 0 and SSM_WIDTH % GLU_TILE == 0


def _mixout_body(at_ref, yg_ref, wg_ref, bg_ref, gs_ref, wo_ref, o_ref):
    at = jnp.concatenate([at_ref[pl.ds(j, SSM_WIDTH, stride=MIX_CHUNKS), :]
                          for j in range(MIX_CHUNKS)], axis=1).astype(BF16)

    def glu_half(r0):
        return jnp.concatenate(
            [_dot(wg_ref[r0 + q * GLU_TILE:r0 + (q + 1) * GLU_TILE, q * GLU_TILE:(q + 1) * GLU_TILE],
                  at[q * GLU_TILE:(q + 1) * GLU_TILE])
             for q in range(SSM_WIDTH // GLU_TILE)], axis=0) + bg_ref[r0:r0 + SSM_WIDTH]

    ot = glu_half(0) * jax.nn.sigmoid(glu_half(SSM_WIDTH))
    ms = jnp.mean(ot * ot, axis=0, keepdims=True)
    ot = (ot * lax.rsqrt(ms + EPS) * gs_ref[...]).astype(BF16)
    o_ref[...] = _dot(yg_ref[...], wo_ref[SSM_WIDTH:, :]) + _dot_tn(ot, wo_ref[:SSM_WIDTH, :])


def _mixout(a_t, y_gm, w_glu_t, b_glu, gain_ssm_out, w_out, *, layer):
    bsz, seq, _ = y_gm.shape
    tm = MIX_TOKENS
    tok = lambda b, i: (b, i, 0)
    return pl.pallas_call(
        _mixout_body,
        grid=(bsz, seq // tm),
        in_specs=[pl.BlockSpec((None, None, SSM_WIDTH * MIX_CHUNKS, SSM_T), lambda b, i: (b, i, 0, 0)),
                  pl.BlockSpec((None, tm, GM_WIDTH), tok),
                  _const_spec((2 * SSM_WIDTH, SSM_WIDTH)),
                  _const_spec((2 * SSM_WIDTH, 1)),
                  _const_spec((SSM_WIDTH, 1)),
                  _layer_spec(w_out, layer)],
        out_specs=pl.BlockSpec((None, tm, D_MODEL), tok),
        out_shape=jax.ShapeDtypeStruct((bsz, seq, D_MODEL), F32),
        compiler_params=_params(2),
        name="mixout",
    )(a_t, y_gm, w_glu_t, b_glu, gain_ssm_out, w_out)


def _glu_block_diag_t(glu_w, glu_b):
    g, c = SSM_GROUPS, SSM_CH
    rows = glu_w.reshape(g, c, 2, c).transpose(2, 0, 3, 1).reshape(2 * g * c, c)
    row_group = (jnp.arange(2 * g * c) // c) % g
    col = jnp.arange(g * c)
    dense = jnp.where(row_group[:, None] == (col // c)[None, :], rows[:, col % c], 0.0)
    bias = glu_b.reshape(g, 2, c).transpose(1, 0, 2).reshape(2 * g * c, 1)
    return dense.astype(BF16), bias


def _tile(n, want, quantum):
    t = min(want, n)
    assert n % t == 0 and t % quantum == 0
    return t


def kernel(x, norm_ffn1, ffn1_w_in, ffn1_w_out, norm_mix, mix_w_in, ssm_a_re, ssm_a_im, ssm_log_dt, ssm_b_re, ssm_b_im, ssm_c_re, ssm_c_im, ssm_d, ssm_glu_w, ssm_glu_b, gm_v_gain, gm_w_s, gm_b_s, gain_ssm_out, gain_gm_out, mix_w_out, norm_ffn2, ffn2_w_in, ffn2_w_out, norm_final):
    bsz, seq, _ = x.shape
    depth = norm_ffn1.shape[0]
    n = bsz * seq
    assert seq % MIX_TOKENS == 0 and bsz % 8 == 0
    tm_ffn = _tile(n, 1024, 8)
    g = SSM_GROUPS
    gain_final = norm_final.reshape(1, D_MODEL)
    flat = lambda a: a.reshape(n, D_MODEL)
    seqs = lambda a: a.reshape(bsz, seq, D_MODEL)

    ffn1_in, ffn1_out = ffn1_w_in.astype(BF16), ffn1_w_out.astype(BF16)
    ffn2_in, ffn2_out = ffn2_w_in.astype(BF16), ffn2_w_out.astype(BF16)
    mix_in, mix_out = mix_w_in.astype(BF16), mix_w_out.astype(BF16)
    mix_in_ssm_t = jnp.swapaxes(mix_in[:, :, :SSM_WIDTH], 1, 2)

    h = flat(x)
    for l in range(depth):
        h = _ffn(h, None, norm_ffn1[l].reshape(1, D_MODEL), ffn1_in, ffn1_out, gain_final,
                 layer=l, final_norm=False, tm=tm_ffn)

        u_t, y_gm = _mixin(seqs(h), norm_mix[l].reshape(1, D_MODEL), mix_in_ssm_t, mix_in,
                           gm_v_gain[l].reshape(1, GM_WIDTH), gm_w_s[l],
                           gm_b_s[l].reshape(GM_HEADS, GM_CHUNK, 1),
                           gain_gm_out[l].reshape(1, GM_WIDTH), layer=l)

        ssm_params = _ssm_param_views(ssm_log_dt[l], ssm_a_re[l], ssm_a_im[l], ssm_b_re[l],
                                      ssm_b_im[l], ssm_c_re[l], ssm_c_im[l])
        d_exp = jnp.repeat(ssm_d[l], SSM_T, axis=1).reshape(g, 1, SSM_K)
        a_t = _ssm(u_t, ssm_params, d_exp)

        w_glu_t, b_glu = _glu_block_diag_t(ssm_glu_w[l], ssm_glu_b[l])
        delta = _mixout(a_t, y_gm, w_glu_t, b_glu,
                        gain_ssm_out[l].reshape(SSM_WIDTH, 1), mix_out, layer=l)

        h = _ffn(h, flat(delta), norm_ffn2[l].reshape(1, D_MODEL), ffn2_in, ffn2_out, gain_final,
                 layer=l, final_norm=(l == depth - 1), tm=tm_ffn)
    return seqs(h)
```

```python
import functools

import jax
import jax.numpy as jnp
from jax import lax
from jax.experimental import pallas as pl
from jax.experimental.pallas import tpu as pltpu

D_MODEL = 1024
SSM_WIDTH = 512
SSM_CH = 16
SSM_GROUPS = 32
SSM_STATE = 64
GM_WIDTH = 512
GM_HEADS = 4
GM_HEAD_DIM = 128
GM_CHUNK = 128
D_FF = 2816
IN_COLS = SSM_WIDTH + 2 * GM_WIDTH
EPS = 1e-6

SSM_T = 128
SSM_K = SSM_CH * SSM_T
MIX_CHUNKS = 8
MIX_TOKENS = MIX_CHUNKS * SSM_T
FF_CHUNKS = (1024, 1024, 768)
assert sum(FF_CHUNKS) == D_FF
VMEM_LIMIT_BYTES = 56 * 1024 * 1024
MIX_SCHEDULER_FLAGS = None

F32 = jnp.float32
BF16 = jnp.bfloat16
I32 = jnp.int32


def _rms(x, g):
    return x * lax.rsqrt(jnp.mean(x * x, axis=-1, keepdims=True) + EPS) * g


def _dot(a, b):
    return jnp.dot(a, b, preferred_element_type=F32)


def _dot_nt(a, b):
    return lax.dot_general(a, b, (((1,), (1,)), ((), ())), preferred_element_type=F32)


def _dot_tn(a, b):
    return lax.dot_general(a, b, (((0,), (0,)), ((), ())), preferred_element_type=F32)


def _const_spec(shape):
    return pl.BlockSpec(shape, lambda *_: (0,) * len(shape), pipeline_mode=pl.Buffered(1))


def _layer_spec(stacked, layer):
    rest = stacked.shape[1:]
    return pl.BlockSpec((None,) + rest, lambda *_: (layer,) + (0,) * len(rest),
                        pipeline_mode=pl.Buffered(1))


def _params(n_axes=1, flags=None):
    return pltpu.CompilerParams(dimension_semantics=("arbitrary",) * n_axes,
                                vmem_limit_bytes=VMEM_LIMIT_BYTES, flags=flags)


def _ffn_body(*refs, has_delta, builds_m, final_norm):
    refs = list(refs)
    x_ref = refs.pop(0)
    x = x_ref[...]
    if has_delta:
        x = x + refs.pop(0)[...]
    g_ref, win_ref, wout_ref, gf_ref = refs[:4]
    refs = refs[4:]
    if builds_m:
        param_refs, refs = refs[:N_SSM_PARAMS], refs[N_SSM_PARAMS:]
        o_ref, m_ref, acc_ref, kt_scr, lam_scr = refs
        taps, _, emit_m = _ssm_operator_emitter(param_refs, kt_scr, lam_scr)
        first_channel = lax.rem(pl.program_id(0), M_PARTS) * M_PART_CHANNELS
        todo = list(range(M_PART_CHANNELS))
    else:
        o_ref, acc_ref = refs
        todo = []
    h = _rms(x, g_ref[...]).astype(BF16)
    lo = 0
    for n_chunk, width in enumerate(FF_CHUNKS):
        gate = _dot(h, win_ref[:, lo:lo + width])
        if builds_m and lo == 0:
            taps()
        up = _dot(h, win_ref[:, D_FF + lo:D_FF + lo + width])
        act = (gate * jax.nn.sigmoid(gate) * up).astype(BF16)
        part = _dot(act, wout_ref[lo:lo + width, :])
        if lo == 0:
            acc_ref[...] = part
        else:
            acc_ref[...] += part
        lo += width
        for i in todo[n_chunk::len(FF_CHUNKS)]:
            emit_m(first_channel + i, m_ref, i)
    y = x + 0.5 * acc_ref[...]
    if final_norm:
        y = _rms(y, gf_ref[...])
    o_ref[...] = y


M_PARTS = 2
M_PART_CHANNELS = SSM_CH // M_PARTS


def _ffn(x, delta, gain, w_in, w_out, gain_final, *, layer, final_norm, tm, ssm_params=None):
    n = x.shape[0]
    row = lambda i: (i, 0)
    streams = [x] if delta is None else [x, delta]
    builds_m = ssm_params is not None
    in_specs = [pl.BlockSpec((tm, D_MODEL), row) for _ in streams] + [
        _const_spec((1, D_MODEL)), _layer_spec(w_in, layer), _layer_spec(w_out, layer),
        _const_spec((1, D_MODEL))]
    out_specs = pl.BlockSpec((tm, D_MODEL), row)
    out_shape = jax.ShapeDtypeStruct((n, D_MODEL), F32)
    scratch = [pltpu.VMEM((tm, D_MODEL), F32)]
    extra = []
    if builds_m:
        assert n // tm == SSM_GROUPS * M_PARTS
        extra = list(ssm_params)
        in_specs += [pl.BlockSpec((1,) + a.shape[1:], lambda i: (i // M_PARTS, 0, 0)) for a in extra]
        out_specs = [out_specs, pl.BlockSpec((None, SSM_K, M_PART_CHANNELS * SSM_T),
                                             lambda i: (i // M_PARTS, 0, i % M_PARTS))]
        out_shape = [out_shape, jax.ShapeDtypeStruct((SSM_GROUPS, SSM_K, SSM_K), BF16)]
        scratch += [pltpu.VMEM((SSM_CH * SSM_CH, SSM_T), F32),
                    pltpu.VMEM((8, 2 * SSM_STATE), F32)]
    return pl.pallas_call(
        functools.partial(_ffn_body, has_delta=delta is not None, builds_m=builds_m,
                          final_norm=final_norm),
        grid=(n // tm,),
        in_specs=in_specs,
        out_specs=out_specs,
        out_shape=out_shape,
        scratch_shapes=scratch,
        compiler_params=_params(),
        name="ffn",
    )(*streams, gain, w_in, w_out, gain_final, *extra)


def _mixin_body(x_ref, g_ref, wu_ref, w_ref, vg_ref, ws_ref, bs_ref, go_ref, ut_ref, yg_ref,
                h_scr, z_scr, y_scr, *, tm):
    pair = 2 * GM_CHUNK
    for r0 in range(0, tm, pair):
        h_scr[r0:r0 + pair, :] = _rms(x_ref[r0:r0 + pair, :], g_ref[...]).astype(BF16)
    z_scr[...] = _dot(h_scr[...], w_ref[:, SSM_WIDTH:])
    ut = _dot_nt(wu_ref[...], h_scr[...])
    for j in range(MIX_CHUNKS):
        ut_ref[pl.ds(j, SSM_WIDTH, stride=MIX_CHUNKS), :] = ut[:, j * SSM_T:(j + 1) * SSM_T]
    t_idx = lax.broadcasted_iota(I32, (GM_CHUNK, GM_CHUNK), 0)
    s_idx = lax.broadcasted_iota(I32, (GM_CHUNK, GM_CHUNK), 1)
    causal = s_idx <= t_idx
    for hd in range(GM_HEADS):
        lo = hd * GM_HEAD_DIM
        ws = jnp.where(causal, ws_ref[hd], 0.0).astype(BF16)
        for r0 in range(0, tm, pair):
            r1, r2 = r0 + GM_CHUNK, r0 + pair
            u = jax.nn.gelu(z_scr[r0:r2, lo:lo + GM_HEAD_DIM])
            v = jax.nn.gelu(z_scr[r0:r2, GM_WIDTH + lo:GM_WIDTH + lo + GM_HEAD_DIM])
            v = _rms(v, vg_ref[:, lo:lo + GM_HEAD_DIM]).astype(BF16)
            s = _dot(ws, jnp.concatenate([v[:GM_CHUNK], v[GM_CHUNK:]], axis=1)) + bs_ref[hd]
            y_scr[r0:r1, lo:lo + GM_HEAD_DIM] = u[:GM_CHUNK] * s[:, :GM_HEAD_DIM]
            y_scr[r1:r2, lo:lo + GM_HEAD_DIM] = u[GM_CHUNK:] * s[:, GM_HEAD_DIM:]
    for r0 in range(0, tm, pair):
        yg_ref[r0:r0 + pair, :] = _rms(y_scr[r0:r0 + pair, :], go_ref[...]).astype(BF16)


def _mixin(x, gain, wu_t, w_in, v_gain, w_s, b_s, gain_gm_out, *, layer):
    bsz, seq, _ = x.shape
    tm = MIX_TOKENS
    tok = lambda b, i: (b, i, 0)
    act_spec = pl.BlockSpec((None, None, SSM_WIDTH * MIX_CHUNKS, SSM_T), lambda b, i: (b, i, 0, 0))
    return pl.pallas_call(
        functools.partial(_mixin_body, tm=tm),
        grid=(bsz, seq // tm),
        in_specs=[pl.BlockSpec((None, tm, D_MODEL), tok),
                  _const_spec((1, D_MODEL)),
                  _layer_spec(wu_t, layer),
                  _layer_spec(w_in, layer),
                  _const_spec((1, GM_WIDTH)),
                  _const_spec((GM_HEADS, GM_CHUNK, GM_CHUNK)),
                  _const_spec((GM_HEADS, GM_CHUNK, 1)),
                  _const_spec((1, GM_WIDTH))],
        out_specs=[act_spec, pl.BlockSpec((None, tm, GM_WIDTH), tok)],
        out_shape=[jax.ShapeDtypeStruct((bsz, seq // tm, SSM_WIDTH * MIX_CHUNKS, SSM_T), F32),
                   jax.ShapeDtypeStruct((bsz, seq, GM_WIDTH), BF16)],
        scratch_shapes=[pltpu.VMEM((tm, D_MODEL), BF16),
                        pltpu.VMEM((tm, 2 * GM_WIDTH), F32),
                        pltpu.VMEM((tm, GM_WIDTH), F32)],
        compiler_params=_params(2, MIX_SCHEDULER_FLAGS),
        name="mixin_gmlp",
    )(x, gain, wu_t, w_in, v_gain, w_s, b_s, gain_gm_out)


def _cmul(ar, ai, br, bi):
    return ar * br - ai * bi, ar * bi + ai * br


N_SSM_PARAMS = 15


def _ssm_operator_emitter(param_refs, kt_scr, lam_ref):
    (ldt_ref, arc_ref, aic_ref, arr_ref, air_ref, ar2_ref, ai2_ref, cx_re_ref, cx_im_ref,
     bx_re_ref, bx_im_ref, ct_re_ref, ct_im_ref, b_re_ref, b_im_ref) = param_refs
    dt = jnp.exp(ldt_ref[0])

    def zoh(a_re, a_im):
        xr, xi = a_re * dt, a_im * dt
        mag = jnp.exp(xr)
        nr, ni = mag * jnp.cos(xi) - 1.0, mag * jnp.sin(xi)
        den = a_re * a_re + a_im * a_im
        return xr, xi, (nr * a_re + ni * a_im) / den, (ni * a_re - nr * a_im) / den

    def powers(xr, xi, k):
        mag = jnp.exp(xr * k)
        return mag * jnp.cos(xi * k), mag * jnp.sin(xi * k)

    xr_c, xi_c, qr_c, qi_c = zoh(arc_ref[0], aic_ref[0])
    k_lane = lax.broadcasted_iota(I32, (SSM_STATE, SSM_T), 1).astype(F32)
    v_re, v_im = powers(xr_c, xi_c, k_lane)
    v1_re, v1_im = powers(xr_c, xi_c, k_lane + 1.0)
    vr_re, vr_im = powers(xr_c, xi_c, (SSM_T - 1.0) - k_lane)

    xr_r, xi_r, qr_r, qi_r = zoh(arr_ref[0], air_ref[0])
    bb_re, bb_im = _cmul(qr_r, qi_r, bx_re_ref[0], bx_im_ref[0])
    w_re, w_im = _cmul(cx_re_ref[0], cx_im_ref[0], bb_re, bb_im)
    exact = functools.partial(jnp.dot, preferred_element_type=F32, precision=lax.Precision.HIGHEST)

    def taps():
        kt_scr[...] = exact(w_re, v_re) - exact(w_im, v_im)

    xr_2, xi_2, _, _ = zoh(ar2_ref[0], ai2_ref[0])
    lt_re, lt_im = powers(xr_2, xi_2, float(SSM_T))
    im_half = lax.broadcasted_iota(I32, (1, 2 * SSM_STATE), 1) >= SSM_STATE
    lam_ref[0:1, :] = lt_re
    lam_ref[1:2, :] = jnp.where(im_half, lt_im, -lt_im)

    s_idx = lax.broadcasted_iota(I32, (SSM_T, SSM_T), 0)
    t_idx = lax.broadcasted_iota(I32, (SSM_T, SSM_T), 1)
    causal = t_idx >= s_idx
    bbc_re, bbc_im = _cmul(qr_c, qi_c, b_re_ref[0], b_im_ref[0])
    ct_re, ct_im = ct_re_ref[0], ct_im_ref[0]

    def emit_m(c, m_ref, col_block):
        cols = slice(col_block * SSM_T, (col_block + 1) * SSM_T)
        for cp in range(SSM_CH):
            taps = jnp.broadcast_to(kt_scr[pl.ds(c * SSM_CH + cp, 1), :], (SSM_T, SSM_T))
            toep = pltpu.roll(taps, 0, 1, stride=1, stride_axis=0)
            m_ref[cp * SSM_T:(cp + 1) * SSM_T, cols] = jnp.where(causal, toep, 0.0).astype(BF16)

    def emit(c, m_ref, et_ref, f_ref):
        cols = slice(c * SSM_T, (c + 1) * SSM_T)
        if m_ref is not None:
            emit_m(c, m_ref, c)
        e_re, e_im = _cmul(bbc_re[:, c:c + 1], bbc_im[:, c:c + 1], vr_re, vr_im)
        et_ref[0:SSM_STATE, cols] = e_re.astype(BF16)
        et_ref[SSM_STATE:2 * SSM_STATE, cols] = e_im.astype(BF16)
        f_re, f_im = _cmul(ct_re[:, c:c + 1], ct_im[:, c:c + 1], v1_re, v1_im)
        f_ref[0:SSM_STATE, cols] = f_re.astype(BF16)
        f_ref[SSM_STATE:2 * SSM_STATE, cols] = (-f_im).astype(BF16)

    return taps, emit, emit_m


def _ssm_param_views(log_dt, a_re, a_im, b_re, b_im, c_re, c_im):
    g, p, c = SSM_GROUPS, SSM_STATE, SSM_CH
    swap = lambda a: jnp.swapaxes(a, 1, 2)
    views = [
        log_dt.reshape(g, 1, 1),
        a_re.reshape(g, p, 1), a_im.reshape(g, p, 1),
        a_re.reshape(g, 1, p), a_im.reshape(g, 1, p),
        jnp.tile(a_re, (1, 2)).reshape(g, 1, 2 * p), jnp.tile(a_im, (1, 2)).reshape(g, 1, 2 * p),
        jnp.repeat(c_re, c, axis=1), jnp.repeat(c_im, c, axis=1),
        jnp.tile(swap(b_re), (1, c, 1)), jnp.tile(swap(b_im), (1, c, 1)),
        swap(c_re), swap(c_im),
        b_re, b_im,
    ]
    assert len(views) == N_SSM_PARAMS
    return views


SSM_N_TILE = 256


def _ssm_body(*refs, n_chunks, bsz):
    param_refs = refs[:N_SSM_PARAMS]
    u_ref, d_ref, m_ref, y_ref = refs[N_SSM_PARAMS:N_SSM_PARAMS + 4]
    slots = refs[N_SSM_PARAMS + 4:-3]
    slot_a, slot_b = slots[:len(slots) // 2], slots[len(slots) // 2:]
    kt_scr, end_scr, h_scr = refs[-3:]
    rows = bsz * n_chunks
    slab_shape = (bsz, n_chunks // MIX_CHUNKS, MIX_CHUNKS, SSM_T)
    n_tiles = SSM_K // SSM_N_TILE
    step = pl.program_id(0)

    @pl.when(step == 0)
    def _zero_what_is_read_before_written():
        for ref in slot_b[:-1] + (slot_a[-1],):
            ref[...] = jnp.zeros(ref.shape, ref.dtype)

    def one_step(new, old):
        new_lhs, new_et, new_f, new_lam, new_raw = new
        old_lhs, old_et, old_f, old_lam, old_raw = old
        lam_a, lam_b = old_lam[0:1, :], old_lam[1:2, :]
        h = jnp.zeros((bsz, 2 * SSM_STATE), F32)
        _, emit, _ = _ssm_operator_emitter(param_refs, kt_scr, new_lam)

        end_scr[...] = _dot_nt(old_lhs[...], old_et[...])
        for j in range(n_tiles):
            cols = slice(j * SSM_N_TILE, (j + 1) * SSM_N_TILE)

            old_raw[:, cols] = _dot(old_lhs[...], m_ref[0, :, cols])
            for n in range(j * n_chunks // n_tiles, (j + 1) * n_chunks // n_tiles):
                of_chunk = pl.ds(n, bsz, stride=n_chunks)
                h_scr[of_chunk, :] = h
                h = h * lam_a + pltpu.roll(h, SSM_STATE, 1) * lam_b + end_scr[of_chunk, :]

            for c in range(j * SSM_CH // n_tiles, (j + 1) * SSM_CH // n_tiles):
                slab = slice(c * SSM_T, (c + 1) * SSM_T)
                of_channel = slice(c * MIX_CHUNKS, (c + 1) * MIX_CHUNKS)
                y_ref[:, :, of_channel, :] = jax.nn.gelu(new_raw[:, slab]).reshape(slab_shape)
                new_lhs[:, slab] = u_ref[:, :, of_channel, :].reshape(rows, SSM_T).astype(BF16)
                emit(c, None, new_et, new_f)

        h_in = h_scr[...].astype(BF16)
        for j in range(n_tiles):
            cols = slice(j * SSM_N_TILE, (j + 1) * SSM_N_TILE)
            old_raw[:, cols] += (_dot(h_in, old_f[:, cols])
                                 + d_ref[0, :, cols] * old_lhs[:, cols].astype(F32))

    parity = lax.rem(step, 2)
    pl.when(parity == 0)(lambda: one_step(slot_a, slot_b))
    pl.when(parity == 1)(lambda: one_step(slot_b, slot_a))


def _ssm(u, params, d_exp, m):
    bsz, n_mix_tiles = u.shape[:2]
    n_chunks = n_mix_tiles * MIX_CHUNKS
    g = SSM_GROUPS
    rows = n_chunks * bsz
    lag = lambda k: (lambda s: (jnp.clip(s - k, 0, g - 1), 0, 0))
    spec = lambda a, k: pl.BlockSpec((1,) + a.shape[1:], lag(k))
    act_spec = lambda k: pl.BlockSpec((bsz, n_mix_tiles, SSM_CH * MIX_CHUNKS, SSM_T),
                                      lambda s: (0, 0, jnp.clip(s - k, 0, g - 1), 0))
    slot = [pltpu.VMEM((rows, SSM_K), BF16),
            pltpu.VMEM((2 * SSM_STATE, SSM_K), BF16),
            pltpu.VMEM((2 * SSM_STATE, SSM_K), BF16),
            pltpu.VMEM((8, 2 * SSM_STATE), F32),
            pltpu.VMEM((rows, SSM_K), F32)]
    return pl.pallas_call(
        functools.partial(_ssm_body, n_chunks=n_chunks, bsz=bsz),
        grid=(g + 2,),
        in_specs=[spec(a, 0) for a in params] + [act_spec(0), spec(d_exp, 1), spec(m, 1)],
        out_specs=act_spec(2),
        out_shape=jax.ShapeDtypeStruct(u.shape, F32),
        scratch_shapes=slot + slot + [pltpu.VMEM((SSM_CH * SSM_CH, SSM_T), F32),
                                      pltpu.VMEM((rows, 2 * SSM_STATE), F32),
                                      pltpu.VMEM((rows, 2 * SSM_STATE), F32)],
        compiler_params=_params(),
        name="ssm",
    )(*params, u, d_exp, m)


GLU_TILE = 256
assert GLU_TILE % SSM_CH == 0 and SSM_WIDTH % GLU_TILE == 0


def _mixout_body(at_ref, yg_ref, wg_ref, bg_ref, gs_ref, wo_ref, o_ref):
    at = jnp.concatenate([at_ref[pl.ds(j, SSM_WIDTH, stride=MIX_CHUNKS), :]
                          for j in range(MIX_CHUNKS)], axis=1).astype(BF16)

    def glu_half(r0):
        return jnp.concatenate(
            [_dot(wg_ref[r0 + q * GLU_TILE:r0 + (q + 1) * GLU_TILE, q * GLU_TILE:(q + 1) * GLU_TILE],
                  at[q * GLU_TILE:(q + 1) * GLU_TILE])
             for q in range(SSM_WIDTH // GLU_TILE)], axis=0) + bg_ref[r0:r0 + SSM_WIDTH]

    ot = glu_half(0) * jax.nn.sigmoid(glu_half(SSM_WIDTH))
    ms = jnp.mean(ot * ot, axis=0, keepdims=True)
    ot = (ot * lax.rsqrt(ms + EPS) * gs_ref[...]).astype(BF16)
    o_ref[...] = _dot(yg_ref[...], wo_ref[SSM_WIDTH:, :]) + _dot_tn(ot, wo_ref[:SSM_WIDTH, :])


def _mixout(a_t, y_gm, w_glu_t, b_glu, gain_ssm_out, w_out, *, layer):
    bsz, seq, _ = y_gm.shape
    tm = MIX_TOKENS
    tok = lambda b, i: (b, i, 0)
    return pl.pallas_call(
        _mixout_body,
        grid=(bsz, seq // tm),
        in_specs=[pl.BlockSpec((None, None, SSM_WIDTH * MIX_CHUNKS, SSM_T), lambda b, i: (b, i, 0, 0)),
                  pl.BlockSpec((None, tm, GM_WIDTH), tok),
                  _const_spec((2 * SSM_WIDTH, SSM_WIDTH)),
                  _const_spec((2 * SSM_WIDTH, 1)),
                  _const_spec((SSM_WIDTH, 1)),
                  _layer_spec(w_out, layer)],
        out_specs=pl.BlockSpec((None, tm, D_MODEL), tok),
        out_shape=jax.ShapeDtypeStruct((bsz, seq, D_MODEL), F32),
        compiler_params=_params(2, MIX_SCHEDULER_FLAGS),
        name="mixout",
    )(a_t, y_gm, w_glu_t, b_glu, gain_ssm_out, w_out)


def _glu_block_diag_t(glu_w, glu_b):
    g, c = SSM_GROUPS, SSM_CH
    rows = glu_w.reshape(g, c, 2, c).transpose(2, 0, 3, 1).reshape(2 * g * c, c)
    row_group = (jnp.arange(2 * g * c) // c) % g
    col = jnp.arange(g * c)
    dense = jnp.where(row_group[:, None] == (col // c)[None, :], rows[:, col % c], 0.0)
    bias = glu_b.reshape(g, 2, c).transpose(1, 0, 2).reshape(2 * g * c, 1)
    return dense.astype(BF16), bias


def _tile(n, want, quantum):
    t = min(want, n)
    assert n % t == 0 and t % quantum == 0
    return t


def kernel(x, norm_ffn1, ffn1_w_in, ffn1_w_out, norm_mix, mix_w_in, ssm_a_re, ssm_a_im, ssm_log_dt, ssm_b_re, ssm_b_im, ssm_c_re, ssm_c_im, ssm_d, ssm_glu_w, ssm_glu_b, gm_v_gain, gm_w_s, gm_b_s, gain_ssm_out, gain_gm_out, mix_w_out, norm_ffn2, ffn2_w_in, ffn2_w_out, norm_final):
    bsz, seq, _ = x.shape
    depth = norm_ffn1.shape[0]
    n = bsz * seq
    assert seq % MIX_TOKENS == 0 and bsz % 8 == 0
    tm_ffn = _tile(n, 1024, 8)
    g = SSM_GROUPS
    gain_final = norm_final.reshape(1, D_MODEL)
    flat = lambda a: a.reshape(n, D_MODEL)
    seqs = lambda a: a.reshape(bsz, seq, D_MODEL)

    ffn1_in, ffn1_out = ffn1_w_in.astype(BF16), ffn1_w_out.astype(BF16)
    ffn2_in, ffn2_out = ffn2_w_in.astype(BF16), ffn2_w_out.astype(BF16)
    mix_in, mix_out = mix_w_in.astype(BF16), mix_w_out.astype(BF16)
    mix_in_ssm_t = jnp.swapaxes(mix_in[:, :, :SSM_WIDTH], 1, 2)

    h = flat(x)
    for l in range(depth):
        ssm_params = _ssm_param_views(ssm_log_dt[l], ssm_a_re[l], ssm_a_im[l], ssm_b_re[l],
                                      ssm_b_im[l], ssm_c_re[l], ssm_c_im[l])
        h, m = _ffn(h, None, norm_ffn1[l].reshape(1, D_MODEL), ffn1_in, ffn1_out, gain_final,
                    layer=l, final_norm=False, tm=tm_ffn, ssm_params=ssm_params)

        u_t, y_gm = _mixin(seqs(h), norm_mix[l].reshape(1, D_MODEL), mix_in_ssm_t, mix_in,
                           gm_v_gain[l].reshape(1, GM_WIDTH), gm_w_s[l],
                           gm_b_s[l].reshape(GM_HEADS, GM_CHUNK, 1),
                           gain_gm_out[l].reshape(1, GM_WIDTH), layer=l)

        d_exp = jnp.repeat(ssm_d[l], SSM_T, axis=1).reshape(g, 1, SSM_K)
        a_t = _ssm(u_t, ssm_params, d_exp, m)

        w_glu_t, b_glu = _glu_block_diag_t(ssm_glu_w[l], ssm_glu_b[l])
        delta = _mixout(a_t, y_gm, w_glu_t, b_glu,
                        gain_ssm_out[l].reshape(SSM_WIDTH, 1), mix_out, layer=l)

        h = _ffn(h, flat(delta), norm_ffn2[l].reshape(1, D_MODEL), ffn2_in, ffn2_out, gain_final,
                 layer=l, final_norm=(l == depth - 1), tm=tm_ffn)
    return seqs(h)
```

```python
import functools

import jax
import jax.numpy as jnp
from jax import lax
from jax.experimental import pallas as pl
from jax.experimental.pallas import tpu as pltpu

D_MODEL = 1024
SSM_WIDTH = 512
SSM_CH = 16
SSM_GROUPS = 32
SSM_STATE = 64
GM_WIDTH = 512
GM_HEADS = 4
GM_HEAD_DIM = 128
GM_CHUNK = 128
D_FF = 2816
IN_COLS = SSM_WIDTH + 2 * GM_WIDTH
EPS = 1e-6

SSM_T = 128
SSM_K = SSM_CH * SSM_T
SUBLANES = 8
MIX_CHUNKS = SUBLANES
MIX_TOKENS = MIX_CHUNKS * SSM_T
FFN_TOKENS = 1024
FF_CHUNKS = (1024, 1024, 768)
assert sum(FF_CHUNKS) == D_FF
VMEM_LIMIT_BYTES = 56 * 1024 * 1024

F32 = jnp.float32
BF16 = jnp.bfloat16
I32 = jnp.int32


def _rms(x, g):
    return x * lax.rsqrt(jnp.mean(x * x, axis=-1, keepdims=True) + EPS) * g


def _dot(a, b):
    return jnp.dot(a, b, preferred_element_type=F32)


def _dot_nt(a, b):
    return lax.dot_general(a, b, (((1,), (1,)), ((), ())), preferred_element_type=F32)


def _dot_tn(a, b):
    return lax.dot_general(a, b, (((0,), (0,)), ((), ())), preferred_element_type=F32)


def _const_spec(shape):
    return pl.BlockSpec(shape, lambda *_: (0,) * len(shape), pipeline_mode=pl.Buffered(1))


def _layer_spec(stacked, layer):
    rest = stacked.shape[1:]
    return pl.BlockSpec((None,) + rest, lambda *_: (layer,) + (0,) * len(rest),
                        pipeline_mode=pl.Buffered(1))


def _params(n_axes=1):
    return pltpu.CompilerParams(dimension_semantics=("arbitrary",) * n_axes,
                                vmem_limit_bytes=VMEM_LIMIT_BYTES)


def _ffn_body(*refs, has_delta, final_norm):
    if has_delta:
        x_ref, d_ref, g_ref, win_ref, wout_ref, gf_ref, o_ref, acc_ref = refs
        x = x_ref[...] + d_ref[...]
    else:
        x_ref, g_ref, win_ref, wout_ref, gf_ref, o_ref, acc_ref = refs
        x = x_ref[...]
    h = _rms(x, g_ref[...]).astype(BF16)
    lo = 0
    for width in FF_CHUNKS:
        gate = _dot(h, win_ref[:, lo:lo + width])
        up = _dot(h, win_ref[:, D_FF + lo:D_FF + lo + width])
        act = (gate * jax.nn.sigmoid(gate) * up).astype(BF16)
        part = _dot(act, wout_ref[lo:lo + width, :])
        if lo == 0:
            acc_ref[...] = part
        else:
            acc_ref[...] += part
        lo += width
    y = x + 0.5 * acc_ref[...]
    if final_norm:
        y = _rms(y, gf_ref[...])
    o_ref[...] = y


def _ffn(x, delta, gain, w_in, w_out, gain_final, *, layer, final_norm, tm):
    n = x.shape[0]
    row = lambda i: (i, 0)
    streams = [x] if delta is None else [x, delta]
    return pl.pallas_call(
        functools.partial(_ffn_body, has_delta=delta is not None, final_norm=final_norm),
        grid=(n // tm,),
        in_specs=[pl.BlockSpec((tm, D_MODEL), row) for _ in streams] + [
                  _const_spec((1, D_MODEL)),
                  _layer_spec(w_in, layer),
                  _layer_spec(w_out, layer),
                  _const_spec((1, D_MODEL))],
        out_specs=pl.BlockSpec((tm, D_MODEL), row),
        out_shape=jax.ShapeDtypeStruct((n, D_MODEL), F32),
        scratch_shapes=[pltpu.VMEM((tm, D_MODEL), F32)],
        compiler_params=_params(),
        name="ffn",
    )(*streams, gain, w_in, w_out, gain_final)


def _mixin_body(x_ref, g_ref, wu_ref, w_ref, vg_ref, ws_ref, bs_ref, go_ref, ut_ref, yg_ref,
                h_scr, z_scr, y_scr, *, tm):
    pair = 2 * GM_CHUNK
    for r0 in range(0, tm, pair):
        h_scr[r0:r0 + pair, :] = _rms(x_ref[r0:r0 + pair, :], g_ref[...]).astype(BF16)
    z_scr[...] = _dot(h_scr[...], w_ref[:, SSM_WIDTH:])
    ut = _dot_nt(wu_ref[...], h_scr[...])
    for j in range(MIX_CHUNKS):
        ut_ref[pl.ds(j, SSM_WIDTH, stride=MIX_CHUNKS), :] = ut[:, j * SSM_T:(j + 1) * SSM_T]
    t_idx = lax.broadcasted_iota(I32, (GM_CHUNK, GM_CHUNK), 0)
    s_idx = lax.broadcasted_iota(I32, (GM_CHUNK, GM_CHUNK), 1)
    causal = s_idx <= t_idx
    for hd in range(GM_HEADS):
        lo = hd * GM_HEAD_DIM
        ws = jnp.where(causal, ws_ref[hd], 0.0).astype(BF16)
        for r0 in range(0, tm, pair):
            r1, r2 = r0 + GM_CHUNK, r0 + pair
            u = jax.nn.gelu(z_scr[r0:r2, lo:lo + GM_HEAD_DIM])
            v = jax.nn.gelu(z_scr[r0:r2, GM_WIDTH + lo:GM_WIDTH + lo + GM_HEAD_DIM])
            v = _rms(v, vg_ref[:, lo:lo + GM_HEAD_DIM]).astype(BF16)
            s = _dot(ws, jnp.concatenate([v[:GM_CHUNK], v[GM_CHUNK:]], axis=1)) + bs_ref[hd]
            y_scr[r0:r1, lo:lo + GM_HEAD_DIM] = u[:GM_CHUNK] * s[:, :GM_HEAD_DIM]
            y_scr[r1:r2, lo:lo + GM_HEAD_DIM] = u[GM_CHUNK:] * s[:, GM_HEAD_DIM:]
    for r0 in range(0, tm, pair):
        yg_ref[r0:r0 + pair, :] = _rms(y_scr[r0:r0 + pair, :], go_ref[...]).astype(BF16)


def _mixin(x, gain, wu_t, w_in, v_gain, w_s, b_s, gain_gm_out, *, layer):
    bsz, seq, _ = x.shape
    tm = MIX_TOKENS
    tok = lambda b, i: (b, i, 0)
    act_spec = pl.BlockSpec((None, None, SSM_WIDTH * MIX_CHUNKS, SSM_T), lambda b, i: (b, i, 0, 0))
    return pl.pallas_call(
        functools.partial(_mixin_body, tm=tm),
        grid=(bsz, seq // tm),
        in_specs=[pl.BlockSpec((None, tm, D_MODEL), tok),
                  _const_spec((1, D_MODEL)),
                  _layer_spec(wu_t, layer),
                  _layer_spec(w_in, layer),
                  _const_spec((1, GM_WIDTH)),
                  _const_spec((GM_HEADS, GM_CHUNK, GM_CHUNK)),
                  _const_spec((GM_HEADS, GM_CHUNK, 1)),
                  _const_spec((1, GM_WIDTH))],
        out_specs=[act_spec, pl.BlockSpec((None, tm, GM_WIDTH), tok)],
        out_shape=[jax.ShapeDtypeStruct((bsz, seq // tm, SSM_WIDTH * MIX_CHUNKS, SSM_T), F32),
                   jax.ShapeDtypeStruct((bsz, seq, GM_WIDTH), BF16)],
        scratch_shapes=[pltpu.VMEM((tm, D_MODEL), BF16),
                        pltpu.VMEM((tm, 2 * GM_WIDTH), F32),
                        pltpu.VMEM((tm, GM_WIDTH), F32)],
        compiler_params=_params(2),
        name="mixin_gmlp",
    )(x, gain, wu_t, w_in, v_gain, w_s, b_s, gain_gm_out)


def _cmul(ar, ai, br, bi):
    return ar * br - ai * bi, ar * bi + ai * br


N_SSM_PARAMS = 15


def _ssm_operator_emitter(param_refs, kt_scr, lam_ref):
    (ldt_ref, arc_ref, aic_ref, arr_ref, air_ref, ar2_ref, ai2_ref, cx_re_ref, cx_im_ref,
     bx_re_ref, bx_im_ref, ct_re_ref, ct_im_ref, b_re_ref, b_im_ref) = param_refs
    dt = jnp.exp(ldt_ref[0])

    def zoh(a_re, a_im):
        xr, xi = a_re * dt, a_im * dt
        mag = jnp.exp(xr)
        nr, ni = mag * jnp.cos(xi) - 1.0, mag * jnp.sin(xi)
        den = a_re * a_re + a_im * a_im
        return xr, xi, (nr * a_re + ni * a_im) / den, (ni * a_re - nr * a_im) / den

    def powers(xr, xi, k):
        mag = jnp.exp(xr * k)
        return mag * jnp.cos(xi * k), mag * jnp.sin(xi * k)

    xr_c, xi_c, qr_c, qi_c = zoh(arc_ref[0], aic_ref[0])
    k_lane = lax.broadcasted_iota(I32, (SSM_STATE, SSM_T), 1).astype(F32)
    v_re, v_im = powers(xr_c, xi_c, k_lane)
    v1_re, v1_im = powers(xr_c, xi_c, k_lane + 1.0)
    vr_re, vr_im = powers(xr_c, xi_c, (SSM_T - 1.0) - k_lane)

    xr_r, xi_r, qr_r, qi_r = zoh(arr_ref[0], air_ref[0])
    bb_re, bb_im = _cmul(qr_r, qi_r, bx_re_ref[0], bx_im_ref[0])
    w_re, w_im = _cmul(cx_re_ref[0], cx_im_ref[0], bb_re, bb_im)
    exact = functools.partial(jnp.dot, preferred_element_type=F32, precision=lax.Precision.HIGHEST)

    def taps():
        kt_scr[...] = exact(w_re, v_re) - exact(w_im, v_im)

    xr_2, xi_2, _, _ = zoh(ar2_ref[0], ai2_ref[0])
    lt_re, lt_im = powers(xr_2, xi_2, float(SSM_T))
    im_half = lax.broadcasted_iota(I32, (1, 2 * SSM_STATE), 1) >= SSM_STATE
    lam_ref[0:1, :] = lt_re
    lam_ref[1:2, :] = jnp.where(im_half, lt_im, -lt_im)

    s_idx = lax.broadcasted_iota(I32, (SSM_T, SSM_T), 0)
    t_idx = lax.broadcasted_iota(I32, (SSM_T, SSM_T), 1)
    causal = t_idx >= s_idx
    bbc_re, bbc_im = _cmul(qr_c, qi_c, b_re_ref[0], b_im_ref[0])
    ct_re, ct_im = ct_re_ref[0], ct_im_ref[0]

    def emit(c, m_ref, et_ref, f_ref):
        cols = slice(c * SSM_T, (c + 1) * SSM_T)
        for cp in range(SSM_CH):
            j = c * SSM_CH + cp
            taps = jnp.broadcast_to(kt_scr[j:j + 1, :], (SSM_T, SSM_T))
            toep = pltpu.roll(taps, 0, 1, stride=1, stride_axis=0)
            m_ref[cp * SSM_T:(cp + 1) * SSM_T, cols] = jnp.where(causal, toep, 0.0).astype(BF16)
        e_re, e_im = _cmul(bbc_re[:, c:c + 1], bbc_im[:, c:c + 1], vr_re, vr_im)
        et_ref[0:SSM_STATE, cols] = e_re.astype(BF16)
        et_ref[SSM_STATE:2 * SSM_STATE, cols] = e_im.astype(BF16)
        f_re, f_im = _cmul(ct_re[:, c:c + 1], ct_im[:, c:c + 1], v1_re, v1_im)
        f_ref[0:SSM_STATE, cols] = f_re.astype(BF16)
        f_ref[SSM_STATE:2 * SSM_STATE, cols] = (-f_im).astype(BF16)

    return taps, emit


def _ssm_param_views(log_dt, a_re, a_im, b_re, b_im, c_re, c_im):
    g, p, c = SSM_GROUPS, SSM_STATE, SSM_CH
    swap = lambda a: jnp.swapaxes(a, 1, 2)
    views = [
        log_dt.reshape(g, 1, 1),
        a_re.reshape(g, p, 1), a_im.reshape(g, p, 1),
        a_re.reshape(g, 1, p), a_im.reshape(g, 1, p),
        jnp.tile(a_re, (1, 2)).reshape(g, 1, 2 * p), jnp.tile(a_im, (1, 2)).reshape(g, 1, 2 * p),
        jnp.repeat(c_re, c, axis=1), jnp.repeat(c_im, c, axis=1),
        jnp.tile(swap(b_re), (1, c, 1)), jnp.tile(swap(b_im), (1, c, 1)),
        swap(c_re), swap(c_im),
        b_re, b_im,
    ]
    assert len(views) == N_SSM_PARAMS
    return views


SSM_N_TILE = 256


def _ssm_body(*refs, n_chunks, bsz):
    param_refs = refs[:N_SSM_PARAMS]
    u_ref, d_ref, y_ref = refs[N_SSM_PARAMS:N_SSM_PARAMS + 3]
    slots = refs[N_SSM_PARAMS + 3:-3]
    slot_a, slot_b = slots[:len(slots) // 2], slots[len(slots) // 2:]
    kt_scr, end_scr, h_scr = refs[-3:]
    rows = bsz * n_chunks
    slab_shape = (bsz, n_chunks // MIX_CHUNKS, MIX_CHUNKS, SSM_T)
    n_tiles = SSM_K // SSM_N_TILE
    step = pl.program_id(0)

    @pl.when(step == 0)
    def _zero_what_is_read_before_written():
        for ref in slot_b[:-1] + (slot_a[-1],):
            ref[...] = jnp.zeros(ref.shape, ref.dtype)

    def one_step(new, old):
        new_lhs, new_m, new_et, new_f, new_lam, new_raw = new
        old_lhs, old_m, old_et, old_f, old_lam, old_raw = old
        lam_a, lam_b = old_lam[0:1, :], old_lam[1:2, :]
        h = jnp.zeros((bsz, 2 * SSM_STATE), F32)
        taps, emit = _ssm_operator_emitter(param_refs, kt_scr, new_lam)

        end_scr[...] = _dot_nt(old_lhs[...], old_et[...])
        for j in range(n_tiles):
            cols = slice(j * SSM_N_TILE, (j + 1) * SSM_N_TILE)

            old_raw[:, cols] = _dot(old_lhs[...], old_m[:, cols])
            for n in range(j * n_chunks // n_tiles, (j + 1) * n_chunks // n_tiles):
                of_chunk = pl.ds(n, bsz, stride=n_chunks)
                h_scr[of_chunk, :] = h
                h = h * lam_a + pltpu.roll(h, SSM_STATE, 1) * lam_b + end_scr[of_chunk, :]

            if j == 0:
                taps()
            for c in range(j * SSM_CH // n_tiles, (j + 1) * SSM_CH // n_tiles):
                slab = slice(c * SSM_T, (c + 1) * SSM_T)
                of_channel = slice(c * MIX_CHUNKS, (c + 1) * MIX_CHUNKS)
                y_ref[:, :, of_channel, :] = jax.nn.gelu(new_raw[:, slab]).reshape(slab_shape)
                new_lhs[:, slab] = u_ref[:, :, of_channel, :].reshape(rows, SSM_T).astype(BF16)
                emit(c, new_m, new_et, new_f)

        h_in = h_scr[...].astype(BF16)
        for j in range(n_tiles):
            cols = slice(j * SSM_N_TILE, (j + 1) * SSM_N_TILE)
            old_raw[:, cols] += (_dot(h_in, old_f[:, cols])
                                 + d_ref[0, :, cols] * old_lhs[:, cols].astype(F32))

    parity = lax.rem(step, 2)
    pl.when(parity == 0)(lambda: one_step(slot_a, slot_b))
    pl.when(parity == 1)(lambda: one_step(slot_b, slot_a))


def _ssm(u, params, d_exp):
    bsz, n_mix_tiles = u.shape[:2]
    n_chunks = n_mix_tiles * MIX_CHUNKS
    g = SSM_GROUPS
    rows = n_chunks * bsz
    lag = lambda k: (lambda s: (jnp.clip(s - k, 0, g - 1), 0, 0))
    spec = lambda a, k: pl.BlockSpec((1,) + a.shape[1:], lag(k))
    act_spec = lambda k: pl.BlockSpec((bsz, n_mix_tiles, SSM_CH * MIX_CHUNKS, SSM_T),
                                      lambda s: (0, 0, jnp.clip(s - k, 0, g - 1), 0))
    slot = [pltpu.VMEM((rows, SSM_K), BF16),
            pltpu.VMEM((SSM_K, SSM_K), BF16),
            pltpu.VMEM((2 * SSM_STATE, SSM_K), BF16),
            pltpu.VMEM((2 * SSM_STATE, SSM_K), BF16),
            pltpu.VMEM((SUBLANES, 2 * SSM_STATE), F32),
            pltpu.VMEM((rows, SSM_K), F32)]
    return pl.pallas_call(
        functools.partial(_ssm_body, n_chunks=n_chunks, bsz=bsz),
        grid=(g + 2,),
        in_specs=[spec(a, 0) for a in params] + [act_spec(0), spec(d_exp, 1)],
        out_specs=act_spec(2),
        out_shape=jax.ShapeDtypeStruct(u.shape, F32),
        scratch_shapes=slot + slot + [pltpu.VMEM((SSM_CH * SSM_CH, SSM_T), F32),
                                      pltpu.VMEM((rows, 2 * SSM_STATE), F32),
                                      pltpu.VMEM((rows, 2 * SSM_STATE), F32)],
        compiler_params=_params(),
        name="ssm",
    )(*params, u, d_exp)


GLU_TILE = 256
assert GLU_TILE % SSM_CH == 0 and SSM_WIDTH % GLU_TILE == 0


def _mixout_body(at_ref, yg_ref, wg_ref, bg_ref, gs_ref, wo_ref, o_ref):
    at = jnp.concatenate([at_ref[pl.ds(j, SSM_WIDTH, stride=MIX_CHUNKS), :]
                          for j in range(MIX_CHUNKS)], axis=1).astype(BF16)

    def glu_half(r0):
        return jnp.concatenate(
            [_dot(wg_ref[r0 + q * GLU_TILE:r0 + (q + 1) * GLU_TILE, q * GLU_TILE:(q + 1) * GLU_TILE],
                  at[q * GLU_TILE:(q + 1) * GLU_TILE])
             for q in range(SSM_WIDTH // GLU_TILE)], axis=0) + bg_ref[r0:r0 + SSM_WIDTH]

    ot = glu_half(0) * jax.nn.sigmoid(glu_half(SSM_WIDTH))
    ms = jnp.mean(ot * ot, axis=0, keepdims=True)
    ot = (ot * lax.rsqrt(ms + EPS) * gs_ref[...]).astype(BF16)
    o_ref[...] = _dot(yg_ref[...], wo_ref[SSM_WIDTH:, :]) + _dot_tn(ot, wo_ref[:SSM_WIDTH, :])


def _mixout(a_t, y_gm, w_glu_t, b_glu, gain_ssm_out, w_out, *, layer):
    bsz, seq, _ = y_gm.shape
    tm = MIX_TOKENS
    tok = lambda b, i: (b, i, 0)
    return pl.pallas_call(
        _mixout_body,
        grid=(bsz, seq // tm),
        in_specs=[pl.BlockSpec((None, None, SSM_WIDTH * MIX_CHUNKS, SSM_T), lambda b, i: (b, i, 0, 0)),
                  pl.BlockSpec((None, tm, GM_WIDTH), tok),
                  _const_spec((2 * SSM_WIDTH, SSM_WIDTH)),
                  _const_spec((2 * SSM_WIDTH, 1)),
                  _const_spec((SSM_WIDTH, 1)),
                  _layer_spec(w_out, layer)],
        out_specs=pl.BlockSpec((None, tm, D_MODEL), tok),
        out_shape=jax.ShapeDtypeStruct((bsz, seq, D_MODEL), F32),
        compiler_params=_params(2),
        name="mixout",
    )(a_t, y_gm, w_glu_t, b_glu, gain_ssm_out, w_out)


def _glu_block_diag_t(glu_w, glu_b):
    g, c = SSM_GROUPS, SSM_CH
    rows = glu_w.reshape(g, c, 2, c).transpose(2, 0, 3, 1).reshape(2 * g * c, c)
    row_group = (jnp.arange(2 * g * c) // c) % g
    col = jnp.arange(g * c)
    dense = jnp.where(row_group[:, None] == (col // c)[None, :], rows[:, col % c], 0.0)
    bias = glu_b.reshape(g, 2, c).transpose(1, 0, 2).reshape(2 * g * c, 1)
    return dense.astype(BF16), bias


def _tile(n, want, quantum):
    t = min(want, n)
    assert n % t == 0 and t % quantum == 0
    return t


def kernel(x, norm_ffn1, ffn1_w_in, ffn1_w_out, norm_mix, mix_w_in, ssm_a_re, ssm_a_im, ssm_log_dt, ssm_b_re, ssm_b_im, ssm_c_re, ssm_c_im, ssm_d, ssm_glu_w, ssm_glu_b, gm_v_gain, gm_w_s, gm_b_s, gain_ssm_out, gain_gm_out, mix_w_out, norm_ffn2, ffn2_w_in, ffn2_w_out, norm_final):
    bsz, seq, _ = x.shape
    depth = norm_ffn1.shape[0]
    n = bsz * seq
    assert seq % MIX_TOKENS == 0 and bsz % SUBLANES == 0
    tm_ffn = _tile(n, FFN_TOKENS, SUBLANES)
    g = SSM_GROUPS
    gain_final = norm_final.reshape(1, D_MODEL)
    flat = lambda a: a.reshape(n, D_MODEL)
    seqs = lambda a: a.reshape(bsz, seq, D_MODEL)

    ffn1_in, ffn1_out = ffn1_w_in.astype(BF16), ffn1_w_out.astype(BF16)
    ffn2_in, ffn2_out = ffn2_w_in.astype(BF16), ffn2_w_out.astype(BF16)
    mix_in, mix_out = mix_w_in.astype(BF16), mix_w_out.astype(BF16)
    mix_in_ssm_t = jnp.swapaxes(mix_in[:, :, :SSM_WIDTH], 1, 2)

    h = flat(x)
    for l in range(depth):
        h = _ffn(h, None, norm_ffn1[l].reshape(1, D_MODEL), ffn1_in, ffn1_out, gain_final,
                 layer=l, final_norm=False, tm=tm_ffn)

        u_t, y_gm = _mixin(seqs(h), norm_mix[l].reshape(1, D_MODEL), mix_in_ssm_t, mix_in,
                           gm_v_gain[l].reshape(1, GM_WIDTH), gm_w_s[l],
                           gm_b_s[l].reshape(GM_HEADS, GM_CHUNK, 1),
                           gain_gm_out[l].reshape(1, GM_WIDTH), layer=l)

        ssm_params = _ssm_param_views(ssm_log_dt[l], ssm_a_re[l], ssm_a_im[l], ssm_b_re[l],
                                      ssm_b_im[l], ssm_c_re[l], ssm_c_im[l])
        d_exp = jnp.repeat(ssm_d[l], SSM_T, axis=1).reshape(g, 1, SSM_K)
        a_t = _ssm(u_t, ssm_params, d_exp)

        w_glu_t, b_glu = _glu_block_diag_t(ssm_glu_w[l], ssm_glu_b[l])
        delta = _mixout(a_t, y_gm, w_glu_t, b_glu,
                        gain_ssm_out[l].reshape(SSM_WIDTH, 1), mix_out, layer=l)

        h = _ffn(h, flat(delta), norm_ffn2[l].reshape(1, D_MODEL), ffn2_in, ffn2_out, gain_final,
                 layer=l, final_norm=(l == depth - 1), tm=tm_ffn)
    return seqs(h)
```

```python
import functools

import jax
import jax.numpy as jnp
from jax import lax
from jax.experimental import pallas as pl
from jax.experimental.pallas import tpu as pltpu

D_MODEL = 1024
SSM_WIDTH = 512
SSM_CH = 16
SSM_GROUPS = 32
SSM_STATE = 64
GM_WIDTH = 512
GM_HEADS = 4
GM_HEAD_DIM = 128
GM_CHUNK = 128
D_FF = 2816
IN_COLS = SSM_WIDTH + 2 * GM_WIDTH
EPS = 1e-6

SSM_T = 128
SSM_K = SSM_CH * SSM_T
SUBLANES = 8
MIX_CHUNKS = SUBLANES
MIX_TOKENS = MIX_CHUNKS * SSM_T
FFN_TOKENS = 1024
FF_CHUNKS = (1024, 1024, 768)
assert sum(FF_CHUNKS) == D_FF
VMEM_LIMIT_BYTES = 56 * 1024 * 1024

F32 = jnp.float32
BF16 = jnp.bfloat16
I32 = jnp.int32


def _rms(x, g):
    return x * lax.rsqrt(jnp.mean(x * x, axis=-1, keepdims=True) + EPS) * g


def _dot(a, b):
    return jnp.dot(a, b, preferred_element_type=F32)


def _dot_nt(a, b):
    return lax.dot_general(a, b, (((1,), (1,)), ((), ())), preferred_element_type=F32)


def _dot_tn(a, b):
    return lax.dot_general(a, b, (((0,), (0,)), ((), ())), preferred_element_type=F32)


def _const_spec(shape):
    return pl.BlockSpec(shape, lambda *_: (0,) * len(shape), pipeline_mode=pl.Buffered(1))


def _layer_spec(stacked, layer):
    rest = stacked.shape[1:]
    return pl.BlockSpec((None,) + rest, lambda *_: (layer,) + (0,) * len(rest),
                        pipeline_mode=pl.Buffered(1))


def _params(n_axes=1):
    return pltpu.CompilerParams(dimension_semantics=("arbitrary",) * n_axes,
                                vmem_limit_bytes=VMEM_LIMIT_BYTES)


def _ffn_body(*refs, has_delta, final_norm):
    if has_delta:
        x_ref, d_ref, g_ref, win_ref, wout_ref, gf_ref, o_ref, acc_ref = refs
        x = x_ref[...] + d_ref[...]
    else:
        x_ref, g_ref, win_ref, wout_ref, gf_ref, o_ref, acc_ref = refs
        x = x_ref[...]
    h = _rms(x, g_ref[...]).astype(BF16)
    lo = 0
    for width in FF_CHUNKS:
        gate = _dot(h, win_ref[:, lo:lo + width])
        up = _dot(h, win_ref[:, D_FF + lo:D_FF + lo + width])
        act = (gate * jax.nn.sigmoid(gate) * up).astype(BF16)
        part = _dot(act, wout_ref[lo:lo + width, :])
        if lo == 0:
            acc_ref[...] = part
        else:
            acc_ref[...] += part
        lo += width
    y = x + 0.5 * acc_ref[...]
    if final_norm:
        y = _rms(y, gf_ref[...])
    o_ref[...] = y


def _ffn(x, delta, gain, w_in, w_out, gain_final, *, layer, final_norm, tm):
    n = x.shape[0]
    row = lambda i: (i, 0)
    streams = [x] if delta is None else [x, delta]
    return pl.pallas_call(
        functools.partial(_ffn_body, has_delta=delta is not None, final_norm=final_norm),
        grid=(n // tm,),
        in_specs=[pl.BlockSpec((tm, D_MODEL), row) for _ in streams] + [
                  _layer_spec(gain, layer),
                  _layer_spec(w_in, layer),
                  _layer_spec(w_out, layer),
                  _const_spec((1, D_MODEL))],
        out_specs=pl.BlockSpec((tm, D_MODEL), row),
        out_shape=jax.ShapeDtypeStruct((n, D_MODEL), F32),
        scratch_shapes=[pltpu.VMEM((tm, D_MODEL), F32)],
        compiler_params=_params(),
        name="ffn",
    )(*streams, gain, w_in, w_out, gain_final)


def _mixin_body(x_ref, g_ref, wu_ref, w_ref, vg_ref, ws_ref, bs_ref, go_ref, ut_ref, yg_ref,
                h_scr, z_scr, y_scr, *, tm):
    pair = 2 * GM_CHUNK
    for r0 in range(0, tm, pair):
        h_scr[r0:r0 + pair, :] = _rms(x_ref[r0:r0 + pair, :], g_ref[...]).astype(BF16)
    z_scr[...] = _dot(h_scr[...], w_ref[:, SSM_WIDTH:])
    ut = _dot_nt(wu_ref[...], h_scr[...])
    for j in range(MIX_CHUNKS):
        ut_ref[pl.ds(j, SSM_WIDTH, stride=MIX_CHUNKS), :] = ut[:, j * SSM_T:(j + 1) * SSM_T]
    t_idx = lax.broadcasted_iota(I32, (GM_CHUNK, GM_CHUNK), 0)
    s_idx = lax.broadcasted_iota(I32, (GM_CHUNK, GM_CHUNK), 1)
    causal = s_idx <= t_idx
    for hd in range(GM_HEADS):
        lo = hd * GM_HEAD_DIM
        ws = jnp.where(causal, ws_ref[hd], 0.0).astype(BF16)
        for r0 in range(0, tm, pair):
            r1, r2 = r0 + GM_CHUNK, r0 + pair
            u = jax.nn.gelu(z_scr[r0:r2, lo:lo + GM_HEAD_DIM])
            v = jax.nn.gelu(z_scr[r0:r2, GM_WIDTH + lo:GM_WIDTH + lo + GM_HEAD_DIM])
            v = _rms(v, vg_ref[:, lo:lo + GM_HEAD_DIM]).astype(BF16)
            s = _dot(ws, jnp.concatenate([v[:GM_CHUNK], v[GM_CHUNK:]], axis=1)) + bs_ref[hd]
            y_scr[r0:r1, lo:lo + GM_HEAD_DIM] = u[:GM_CHUNK] * s[:, :GM_HEAD_DIM]
            y_scr[r1:r2, lo:lo + GM_HEAD_DIM] = u[GM_CHUNK:] * s[:, GM_HEAD_DIM:]
    for r0 in range(0, tm, pair):
        yg_ref[r0:r0 + pair, :] = _rms(y_scr[r0:r0 + pair, :], go_ref[...]).astype(BF16)


def _mixin(x, gain, wu_t, w_in, v_gain, w_s, b_s, gain_gm_out, *, layer):
    bsz, seq, _ = x.shape
    tm = MIX_TOKENS
    tok = lambda b, i: (b, i, 0)
    act_spec = pl.BlockSpec((None, None, SSM_WIDTH * MIX_CHUNKS, SSM_T), lambda b, i: (b, i, 0, 0))
    return pl.pallas_call(
        functools.partial(_mixin_body, tm=tm),
        grid=(bsz, seq // tm),
        in_specs=[pl.BlockSpec((None, tm, D_MODEL), tok)] + [
                  _layer_spec(a, layer) for a in (gain, wu_t, w_in, v_gain, w_s, b_s, gain_gm_out)],
        out_specs=[act_spec, pl.BlockSpec((None, tm, GM_WIDTH), tok)],
        out_shape=[jax.ShapeDtypeStruct((bsz, seq // tm, SSM_WIDTH * MIX_CHUNKS, SSM_T), F32),
                   jax.ShapeDtypeStruct((bsz, seq, GM_WIDTH), BF16)],
        scratch_shapes=[pltpu.VMEM((tm, D_MODEL), BF16),
                        pltpu.VMEM((tm, 2 * GM_WIDTH), F32),
                        pltpu.VMEM((tm, GM_WIDTH), F32)],
        compiler_params=_params(2),
        name="mixin_gmlp",
    )(x, gain, wu_t, w_in, v_gain, w_s, b_s, gain_gm_out)


def _cmul(ar, ai, br, bi):
    return ar * br - ai * bi, ar * bi + ai * br


N_SSM_PARAMS = 15


def _ssm_operator_emitter(param_refs, kt_scr, lam_ref):
    (ldt_ref, arc_ref, aic_ref, arr_ref, air_ref, ar2_ref, ai2_ref, cx_re_ref, cx_im_ref,
     bx_re_ref, bx_im_ref, ct_re_ref, ct_im_ref, b_re_ref, b_im_ref) = param_refs
    dt = jnp.exp(ldt_ref[0])

    def zoh(a_re, a_im):
        xr, xi = a_re * dt, a_im * dt
        mag = jnp.exp(xr)
        nr, ni = mag * jnp.cos(xi) - 1.0, mag * jnp.sin(xi)
        den = a_re * a_re + a_im * a_im
        return xr, xi, (nr * a_re + ni * a_im) / den, (ni * a_re - nr * a_im) / den

    def powers(xr, xi, k):
        mag = jnp.exp(xr * k)
        return mag * jnp.cos(xi * k), mag * jnp.sin(xi * k)

    xr_c, xi_c, qr_c, qi_c = zoh(arc_ref[0], aic_ref[0])
    k_lane = lax.broadcasted_iota(I32, (SSM_STATE, SSM_T), 1).astype(F32)
    v_re, v_im = powers(xr_c, xi_c, k_lane)
    v1_re, v1_im = powers(xr_c, xi_c, k_lane + 1.0)
    vr_re, vr_im = powers(xr_c, xi_c, (SSM_T - 1.0) - k_lane)

    xr_r, xi_r, qr_r, qi_r = zoh(arr_ref[0], air_ref[0])
    bb_re, bb_im = _cmul(qr_r, qi_r, bx_re_ref[0], bx_im_ref[0])
    w_re, w_im = _cmul(cx_re_ref[0], cx_im_ref[0], bb_re, bb_im)
    exact = functools.partial(jnp.dot, preferred_element_type=F32, precision=lax.Precision.HIGHEST)

    def taps():
        kt_scr[...] = exact(w_re, v_re) - exact(w_im, v_im)

    xr_2, xi_2, _, _ = zoh(ar2_ref[0], ai2_ref[0])
    lt_re, lt_im = powers(xr_2, xi_2, float(SSM_T))
    im_half = lax.broadcasted_iota(I32, (1, 2 * SSM_STATE), 1) >= SSM_STATE
    lam_ref[0:1, :] = lt_re
    lam_ref[1:2, :] = jnp.where(im_half, lt_im, -lt_im)

    s_idx = lax.broadcasted_iota(I32, (SSM_T, SSM_T), 0)
    t_idx = lax.broadcasted_iota(I32, (SSM_T, SSM_T), 1)
    causal = t_idx >= s_idx
    bbc_re, bbc_im = _cmul(qr_c, qi_c, b_re_ref[0], b_im_ref[0])
    ct_re, ct_im = ct_re_ref[0], ct_im_ref[0]

    def emit(c, m_ref, et_ref, f_ref):
        cols = slice(c * SSM_T, (c + 1) * SSM_T)
        for cp in range(SSM_CH):
            j = c * SSM_CH + cp
            taps = jnp.broadcast_to(kt_scr[j:j + 1, :], (SSM_T, SSM_T))
            toep = pltpu.roll(taps, 0, 1, stride=1, stride_axis=0)
            m_ref[cp * SSM_T:(cp + 1) * SSM_T, cols] = jnp.where(causal, toep, 0.0).astype(BF16)
        e_re, e_im = _cmul(bbc_re[:, c:c + 1], bbc_im[:, c:c + 1], vr_re, vr_im)
        et_ref[0:SSM_STATE, cols] = e_re.astype(BF16)
        et_ref[SSM_STATE:2 * SSM_STATE, cols] = e_im.astype(BF16)
        f_re, f_im = _cmul(ct_re[:, c:c + 1], ct_im[:, c:c + 1], v1_re, v1_im)
        f_ref[0:SSM_STATE, cols] = f_re.astype(BF16)
        f_ref[SSM_STATE:2 * SSM_STATE, cols] = (-f_im).astype(BF16)

    return taps, emit


def _ssm_param_views(log_dt, a_re, a_im, b_re, b_im, c_re, c_im):
    g, p, c = log_dt.shape[0], SSM_STATE, SSM_CH
    swap = lambda a: jnp.swapaxes(a, 1, 2)
    views = [
        log_dt.reshape(g, 1, 1),
        a_re.reshape(g, p, 1), a_im.reshape(g, p, 1),
        a_re.reshape(g, 1, p), a_im.reshape(g, 1, p),
        jnp.tile(a_re, (1, 2)).reshape(g, 1, 2 * p), jnp.tile(a_im, (1, 2)).reshape(g, 1, 2 * p),
        jnp.repeat(c_re, c, axis=1), jnp.repeat(c_im, c, axis=1),
        jnp.tile(swap(b_re), (1, c, 1)), jnp.tile(swap(b_im), (1, c, 1)),
        swap(c_re), swap(c_im),
        b_re, b_im,
    ]
    assert len(views) == N_SSM_PARAMS
    return views


SSM_N_TILE = 256


def _ssm_body(*refs, n_chunks, bsz):
    param_refs = refs[:N_SSM_PARAMS]
    u_ref, d_ref, y_ref = refs[N_SSM_PARAMS:N_SSM_PARAMS + 3]
    slots = refs[N_SSM_PARAMS + 3:-3]
    slot_a, slot_b = slots[:len(slots) // 2], slots[len(slots) // 2:]
    kt_scr, end_scr, h_scr = refs[-3:]
    rows = bsz * n_chunks
    slab_shape = (bsz, n_chunks // MIX_CHUNKS, MIX_CHUNKS, SSM_T)
    n_tiles = SSM_K // SSM_N_TILE
    step = pl.program_id(0)

    @pl.when(step == 0)
    def _zero_what_is_read_before_written():
        for ref in slot_b[:-1] + (slot_a[-1],):
            ref[...] = jnp.zeros(ref.shape, ref.dtype)

    def one_step(new, old):
        new_lhs, new_m, new_et, new_f, new_lam, new_raw = new
        old_lhs, old_m, old_et, old_f, old_lam, old_raw = old
        lam_a, lam_b = old_lam[0:1, :], old_lam[1:2, :]
        h = jnp.zeros((bsz, 2 * SSM_STATE), F32)
        taps, emit = _ssm_operator_emitter(param_refs, kt_scr, new_lam)

        end_scr[...] = _dot_nt(old_lhs[...], old_et[...])
        for j in range(n_tiles):
            cols = slice(j * SSM_N_TILE, (j + 1) * SSM_N_TILE)

            old_raw[:, cols] = _dot(old_lhs[...], old_m[:, cols])
            for n in range(j * n_chunks // n_tiles, (j + 1) * n_chunks // n_tiles):
                of_chunk = pl.ds(n, bsz, stride=n_chunks)
                h_scr[of_chunk, :] = h
                h = h * lam_a + pltpu.roll(h, SSM_STATE, 1) * lam_b + end_scr[of_chunk, :]

            if j == 0:
                taps()
            for c in range(j * SSM_CH // n_tiles, (j + 1) * SSM_CH // n_tiles):
                slab = slice(c * SSM_T, (c + 1) * SSM_T)
                of_channel = slice(c * MIX_CHUNKS, (c + 1) * MIX_CHUNKS)
                y_ref[:, :, of_channel, :] = jax.nn.gelu(new_raw[:, slab]).reshape(slab_shape)
                new_lhs[:, slab] = u_ref[:, :, of_channel, :].reshape(rows, SSM_T).astype(BF16)
                emit(c, new_m, new_et, new_f)

        h_in = h_scr[...].astype(BF16)
        for j in range(n_tiles):
            cols = slice(j * SSM_N_TILE, (j + 1) * SSM_N_TILE)
            old_raw[:, cols] += (_dot(h_in, old_f[:, cols])
                                 + d_ref[0, :, cols] * old_lhs[:, cols].astype(F32))

    parity = lax.rem(step, 2)
    pl.when(parity == 0)(lambda: one_step(slot_a, slot_b))
    pl.when(parity == 1)(lambda: one_step(slot_b, slot_a))


def _ssm(u, params, d_exp, *, layer):
    bsz, n_mix_tiles = u.shape[:2]
    n_chunks = n_mix_tiles * MIX_CHUNKS
    g = SSM_GROUPS
    rows = n_chunks * bsz
    lag = lambda k: (lambda s: (layer * g + jnp.clip(s - k, 0, g - 1), 0, 0))
    spec = lambda a, k: pl.BlockSpec((1,) + a.shape[1:], lag(k))
    act_spec = lambda k: pl.BlockSpec((bsz, n_mix_tiles, SSM_CH * MIX_CHUNKS, SSM_T),
                                      lambda s: (0, 0, jnp.clip(s - k, 0, g - 1), 0))
    slot = [pltpu.VMEM((rows, SSM_K), BF16),
            pltpu.VMEM((SSM_K, SSM_K), BF16),
            pltpu.VMEM((2 * SSM_STATE, SSM_K), BF16),
            pltpu.VMEM((2 * SSM_STATE, SSM_K), BF16),
            pltpu.VMEM((SUBLANES, 2 * SSM_STATE), F32),
            pltpu.VMEM((rows, SSM_K), F32)]
    return pl.pallas_call(
        functools.partial(_ssm_body, n_chunks=n_chunks, bsz=bsz),
        grid=(g + 2,),
        in_specs=[spec(a, 0) for a in params] + [act_spec(0), spec(d_exp, 1)],
        out_specs=act_spec(2),
        out_shape=jax.ShapeDtypeStruct(u.shape, F32),
        scratch_shapes=slot + slot + [pltpu.VMEM((SSM_CH * SSM_CH, SSM_T), F32),
                                      pltpu.VMEM((rows, 2 * SSM_STATE), F32),
                                      pltpu.VMEM((rows, 2 * SSM_STATE), F32)],
        compiler_params=_params(),
        name="ssm",
    )(*params, u, d_exp)


GLU_TILE = 256
assert GLU_TILE % SSM_CH == 0 and SSM_WIDTH % GLU_TILE == 0


def _mixout_body(at_ref, yg_ref, wg_ref, bg_ref, gs_ref, wo_ref, o_ref):
    at = jnp.concatenate([at_ref[pl.ds(j, SSM_WIDTH, stride=MIX_CHUNKS), :]
                          for j in range(MIX_CHUNKS)], axis=1).astype(BF16)

    def glu_half(r0):
        return jnp.concatenate(
            [_dot(wg_ref[r0 + q * GLU_TILE:r0 + (q + 1) * GLU_TILE, q * GLU_TILE:(q + 1) * GLU_TILE],
                  at[q * GLU_TILE:(q + 1) * GLU_TILE])
             for q in range(SSM_WIDTH // GLU_TILE)], axis=0) + bg_ref[r0:r0 + SSM_WIDTH]

    ot = glu_half(0) * jax.nn.sigmoid(glu_half(SSM_WIDTH))
    ms = jnp.mean(ot * ot, axis=0, keepdims=True)
    ot = (ot * lax.rsqrt(ms + EPS) * gs_ref[...]).astype(BF16)
    o_ref[...] = _dot(yg_ref[...], wo_ref[SSM_WIDTH:, :]) + _dot_tn(ot, wo_ref[:SSM_WIDTH, :])


def _mixout(a_t, y_gm, w_glu_t, b_glu, gain_ssm_out, w_out, *, layer):
    bsz, seq, _ = y_gm.shape
    tm = MIX_TOKENS
    tok = lambda b, i: (b, i, 0)
    return pl.pallas_call(
        _mixout_body,
        grid=(bsz, seq // tm),
        in_specs=[pl.BlockSpec((None, None, SSM_WIDTH * MIX_CHUNKS, SSM_T), lambda b, i: (b, i, 0, 0)),
                  pl.BlockSpec((None, tm, GM_WIDTH), tok)] + [
                  _layer_spec(a, layer) for a in (w_glu_t, b_glu, gain_ssm_out, w_out)],
        out_specs=pl.BlockSpec((None, tm, D_MODEL), tok),
        out_shape=jax.ShapeDtypeStruct((bsz, seq, D_MODEL), F32),
        compiler_params=_params(2),
        name="mixout",
    )(a_t, y_gm, w_glu_t, b_glu, gain_ssm_out, w_out)


def _glu_block_diag_t(glu_w, glu_b):
    g, c = SSM_GROUPS, SSM_CH
    rows = glu_w.reshape(g, c, 2, c).transpose(2, 0, 3, 1).reshape(2 * g * c, c)
    row_group = (jnp.arange(2 * g * c) // c) % g
    col = jnp.arange(g * c)
    dense = jnp.where(row_group[:, None] == (col // c)[None, :], rows[:, col % c], 0.0)
    bias = glu_b.reshape(g, 2, c).transpose(1, 0, 2).reshape(2 * g * c, 1)
    return dense.astype(BF16), bias


def _tile(n, want, quantum):
    t = min(want, n)
    assert n % t == 0 and t % quantum == 0
    return t


def kernel(x, norm_ffn1, ffn1_w_in, ffn1_w_out, norm_mix, mix_w_in, ssm_a_re, ssm_a_im, ssm_log_dt, ssm_b_re, ssm_b_im, ssm_c_re, ssm_c_im, ssm_d, ssm_glu_w, ssm_glu_b, gm_v_gain, gm_w_s, gm_b_s, gain_ssm_out, gain_gm_out, mix_w_out, norm_ffn2, ffn2_w_in, ffn2_w_out, norm_final):
    bsz, seq, _ = x.shape
    depth = norm_ffn1.shape[0]
    n = bsz * seq
    assert seq % MIX_TOKENS == 0 and bsz % SUBLANES == 0
    tm_ffn = _tile(n, FFN_TOKENS, SUBLANES)
    g = SSM_GROUPS
    gain_final = norm_final.reshape(1, D_MODEL)
    flat = lambda a: a.reshape(n, D_MODEL)
    seqs = lambda a: a.reshape(bsz, seq, D_MODEL)

    ffn1_in, ffn1_out = ffn1_w_in.astype(BF16), ffn1_w_out.astype(BF16)
    ffn2_in, ffn2_out = ffn2_w_in.astype(BF16), ffn2_w_out.astype(BF16)
    mix_in, mix_out = mix_w_in.astype(BF16), mix_w_out.astype(BF16)
    mix_in_ssm_t = jnp.swapaxes(mix_in[:, :, :SSM_WIDTH], 1, 2)

    row = lambda a: a.reshape(depth, 1, -1)
    col = lambda a: a.reshape(depth, -1, 1)
    all_groups = lambda a: a.reshape((depth * g,) + a.shape[2:])
    ssm_params = _ssm_param_views(*(all_groups(a) for a in (
        ssm_log_dt, ssm_a_re, ssm_a_im, ssm_b_re, ssm_b_im, ssm_c_re, ssm_c_im)))
    d_exp = jnp.repeat(all_groups(ssm_d), SSM_T, axis=1).reshape(depth * g, 1, SSM_K)
    w_glu_t, b_glu = jax.vmap(_glu_block_diag_t)(ssm_glu_w, ssm_glu_b)
    b_s_cols = gm_b_s.reshape(depth, GM_HEADS, GM_CHUNK, 1)

    h = flat(x)
    for l in range(depth):
        h = _ffn(h, None, row(norm_ffn1), ffn1_in, ffn1_out, gain_final,
                 layer=l, final_norm=False, tm=tm_ffn)

        u_t, y_gm = _mixin(seqs(h), row(norm_mix), mix_in_ssm_t, mix_in, row(gm_v_gain), gm_w_s,
                           b_s_cols, row(gain_gm_out), layer=l)

        a_t = _ssm(u_t, ssm_params, d_exp, layer=l)

        delta = _mixout(a_t, y_gm, w_glu_t, b_glu, col(gain_ssm_out), mix_out, layer=l)

        h = _ffn(h, flat(delta), row(norm_ffn2), ffn2_in, ffn2_out, gain_final,
                 layer=l, final_norm=(l == depth - 1), tm=tm_ffn)
    return seqs(h)
```

```python
import functools

import jax
import jax.numpy as jnp
from jax import lax
from jax.experimental import pallas as pl
from jax.experimental.pallas import tpu as pltpu

D_MODEL = 1024
SSM_WIDTH = 512
SSM_CH = 16
SSM_GROUPS = 32
SSM_STATE = 64
GM_WIDTH = 512
GM_HEADS = 4
GM_HEAD_DIM = 128
GM_CHUNK = 128
D_FF = 2816
IN_COLS = SSM_WIDTH + 2 * GM_WIDTH
EPS = 1e-6

SSM_T = 128
SSM_K = SSM_CH * SSM_T
SUBLANES = 8
MIX_CHUNKS = SUBLANES
MIX_TOKENS = MIX_CHUNKS * SSM_T
FFN_TOKENS = 1024
FF_CHUNKS = (1024, 1024, 768)
assert sum(FF_CHUNKS) == D_FF
VMEM_LIMIT_BYTES = 56 * 1024 * 1024

F32 = jnp.float32
BF16 = jnp.bfloat16
I32 = jnp.int32


def _rms(x, g):
    return x * lax.rsqrt(jnp.mean(x * x, axis=-1, keepdims=True) + EPS) * g


def _dot(a, b):
    return jnp.dot(a, b, preferred_element_type=F32)


def _dot_nt(a, b):
    return lax.dot_general(a, b, (((1,), (1,)), ((), ())), preferred_element_type=F32)


def _dot_tn(a, b):
    return lax.dot_general(a, b, (((0,), (0,)), ((), ())), preferred_element_type=F32)


def _const_spec(shape):
    return pl.BlockSpec(shape, lambda *_: (0,) * len(shape), pipeline_mode=pl.Buffered(1))


def _layer_spec(stacked, layer):
    rest = stacked.shape[1:]
    return pl.BlockSpec((None,) + rest, lambda *_: (layer,) + (0,) * len(rest),
                        pipeline_mode=pl.Buffered(1))


def _params(n_axes=1):
    return pltpu.CompilerParams(dimension_semantics=("arbitrary",) * n_axes,
                                vmem_limit_bytes=VMEM_LIMIT_BYTES)


def _ffn_body(*refs, has_delta, final_norm):
    if has_delta:
        x_ref, d_ref, g_ref, win_ref, wout_ref, gf_ref, o_ref, acc_ref = refs
        x = x_ref[...] + d_ref[...]
    else:
        x_ref, g_ref, win_ref, wout_ref, gf_ref, o_ref, acc_ref = refs
        x = x_ref[...]
    h = _rms(x, g_ref[...]).astype(BF16)
    lo = 0
    for width in FF_CHUNKS:
        gate = _dot(h, win_ref[:, lo:lo + width])
        up = _dot(h, win_ref[:, D_FF + lo:D_FF + lo + width])
        act = (gate * jax.nn.sigmoid(gate) * up).astype(BF16)
        part = _dot(act, wout_ref[lo:lo + width, :])
        if lo == 0:
            acc_ref[...] = part
        else:
            acc_ref[...] += part
        lo += width
    y = x + 0.5 * acc_ref[...]
    if final_norm:
        y = _rms(y, gf_ref[...])
    o_ref[...] = y


def _ffn(x, delta, gain, w_in, w_out, gain_final, *, layer, final_norm, tm):
    n = x.shape[0]
    row = lambda i: (i, 0)
    streams = [x] if delta is None else [x, delta]
    return pl.pallas_call(
        functools.partial(_ffn_body, has_delta=delta is not None, final_norm=final_norm),
        grid=(n // tm,),
        in_specs=[pl.BlockSpec((tm, D_MODEL), row) for _ in streams] + [
                  _layer_spec(gain, layer),
                  _layer_spec(w_in, layer),
                  _layer_spec(w_out, layer),
                  _const_spec((1, D_MODEL))],
        out_specs=pl.BlockSpec((tm, D_MODEL), row),
        out_shape=jax.ShapeDtypeStruct((n, D_MODEL), F32),
        scratch_shapes=[pltpu.VMEM((tm, D_MODEL), F32)],
        compiler_params=_params(),
        name="ffn",
    )(*streams, gain, w_in, w_out, gain_final)


def _mixin_body(x_ref, g_ref, wu_ref, w_ref, vg_ref, ws_ref, bs_ref, go_ref, ut_ref, yg_ref,
                h_scr, z_scr, y_scr, *, tm):
    pair = 2 * GM_CHUNK
    for r0 in range(0, tm, pair):
        h_scr[r0:r0 + pair, :] = _rms(x_ref[r0:r0 + pair, :], g_ref[...]).astype(BF16)
    z_scr[...] = _dot(h_scr[...], w_ref[:, SSM_WIDTH:])
    ut = _dot_nt(wu_ref[...], h_scr[...])
    for j in range(MIX_CHUNKS):
        ut_ref[pl.ds(j, SSM_WIDTH, stride=MIX_CHUNKS), :] = ut[:, j * SSM_T:(j + 1) * SSM_T]
    t_idx = lax.broadcasted_iota(I32, (GM_CHUNK, GM_CHUNK), 0)
    s_idx = lax.broadcasted_iota(I32, (GM_CHUNK, GM_CHUNK), 1)
    causal = s_idx <= t_idx
    for hd in range(GM_HEADS):
        lo = hd * GM_HEAD_DIM
        ws = jnp.where(causal, ws_ref[hd], 0.0).astype(BF16)
        for r0 in range(0, tm, pair):
            r1, r2 = r0 + GM_CHUNK, r0 + pair
            u = jax.nn.gelu(z_scr[r0:r2, lo:lo + GM_HEAD_DIM])
            v = jax.nn.gelu(z_scr[r0:r2, GM_WIDTH + lo:GM_WIDTH + lo + GM_HEAD_DIM])
            v = _rms(v, vg_ref[:, lo:lo + GM_HEAD_DIM]).astype(BF16)
            s = _dot(ws, jnp.concatenate([v[:GM_CHUNK], v[GM_CHUNK:]], axis=1)) + bs_ref[hd]
            y_scr[r0:r1, lo:lo + GM_HEAD_DIM] = u[:GM_CHUNK] * s[:, :GM_HEAD_DIM]
            y_scr[r1:r2, lo:lo + GM_HEAD_DIM] = u[GM_CHUNK:] * s[:, GM_HEAD_DIM:]
    for r0 in range(0, tm, pair):
        yg_ref[r0:r0 + pair, :] = _rms(y_scr[r0:r0 + pair, :], go_ref[...]).astype(BF16)


def _mixin(x, gain, wu_t, w_in, v_gain, w_s, b_s, gain_gm_out, *, layer):
    bsz, seq, _ = x.shape
    tm = MIX_TOKENS
    tok = lambda b, i: (b, i, 0)
    act_spec = pl.BlockSpec((None, None, SSM_WIDTH * MIX_CHUNKS, SSM_T), lambda b, i: (b, i, 0, 0))
    return pl.pallas_call(
        functools.partial(_mixin_body, tm=tm),
        grid=(bsz, seq // tm),
        in_specs=[pl.BlockSpec((None, tm, D_MODEL), tok)] + [
                  _layer_spec(a, layer) for a in (gain, wu_t, w_in, v_gain, w_s, b_s, gain_gm_out)],
        out_specs=[act_spec, pl.BlockSpec((None, tm, GM_WIDTH), tok)],
        out_shape=[jax.ShapeDtypeStruct((bsz, seq // tm, SSM_WIDTH * MIX_CHUNKS, SSM_T), F32),
                   jax.ShapeDtypeStruct((bsz, seq, GM_WIDTH), BF16)],
        scratch_shapes=[pltpu.VMEM((tm, D_MODEL), BF16),
                        pltpu.VMEM((tm, 2 * GM_WIDTH), F32),
                        pltpu.VMEM((tm, GM_WIDTH), F32)],
        compiler_params=_params(2),
        name="mixin_gmlp",
    )(x, gain, wu_t, w_in, v_gain, w_s, b_s, gain_gm_out)


def _cmul(ar, ai, br, bi):
    return ar * br - ai * bi, ar * bi + ai * br


N_SSM_PARAMS = 15


def _ssm_operator_emitter(param_refs, kt_scr, lam_ref):
    (ldt_ref, arc_ref, aic_ref, arr_ref, air_ref, ar2_ref, ai2_ref, cx_re_ref, cx_im_ref,
     bx_re_ref, bx_im_ref, ct_re_ref, ct_im_ref, b_re_ref, b_im_ref) = param_refs
    dt = jnp.exp(ldt_ref[0])

    def zoh(a_re, a_im):
        xr, xi = a_re * dt, a_im * dt
        mag = jnp.exp(xr)
        nr, ni = mag * jnp.cos(xi) - 1.0, mag * jnp.sin(xi)
        den = a_re * a_re + a_im * a_im
        return xr, xi, (nr * a_re + ni * a_im) / den, (ni * a_re - nr * a_im) / den

    def powers(xr, xi, k):
        mag = jnp.exp(xr * k)
        return mag * jnp.cos(xi * k), mag * jnp.sin(xi * k)

    xr_c, xi_c, qr_c, qi_c = zoh(arc_ref[0], aic_ref[0])
    k_lane = lax.broadcasted_iota(I32, (SSM_STATE, SSM_T), 1).astype(F32)
    v_re, v_im = powers(xr_c, xi_c, k_lane)
    v1_re, v1_im = powers(xr_c, xi_c, k_lane + 1.0)
    vr_re, vr_im = powers(xr_c, xi_c, (SSM_T - 1.0) - k_lane)

    xr_r, xi_r, qr_r, qi_r = zoh(arr_ref[0], air_ref[0])
    bb_re, bb_im = _cmul(qr_r, qi_r, bx_re_ref[0], bx_im_ref[0])
    w_re, w_im = _cmul(cx_re_ref[0], cx_im_ref[0], bb_re, bb_im)
    exact = functools.partial(jnp.dot, preferred_element_type=F32, precision=lax.Precision.HIGHEST)

    def taps():
        kt_scr[...] = exact(w_re, v_re) - exact(w_im, v_im)

    xr_2, xi_2, _, _ = zoh(ar2_ref[0], ai2_ref[0])
    lt_re, lt_im = powers(xr_2, xi_2, float(SSM_T))
    im_half = lax.broadcasted_iota(I32, (1, 2 * SSM_STATE), 1) >= SSM_STATE
    lam_ref[0:1, :] = lt_re
    lam_ref[1:2, :] = jnp.where(im_half, lt_im, -lt_im)

    s_idx = lax.broadcasted_iota(I32, (SSM_T, SSM_T), 0)
    t_idx = lax.broadcasted_iota(I32, (SSM_T, SSM_T), 1)
    causal = t_idx >= s_idx
    bbc_re, bbc_im = _cmul(qr_c, qi_c, b_re_ref[0], b_im_ref[0])
    ct_re, ct_im = ct_re_ref[0], ct_im_ref[0]

    def emit(c, m_ref, et_ref, f_ref):
        cols = slice(c * SSM_T, (c + 1) * SSM_T)
        for cp in range(SSM_CH):
            j = c * SSM_CH + cp
            taps = jnp.broadcast_to(kt_scr[j:j + 1, :], (SSM_T, SSM_T))
            toep = pltpu.roll(taps, 0, 1, stride=1, stride_axis=0)
            m_ref[cp * SSM_T:(cp + 1) * SSM_T, cols] = jnp.where(causal, toep, 0.0).astype(BF16)
        e_re, e_im = _cmul(bbc_re[:, c:c + 1], bbc_im[:, c:c + 1], vr_re, vr_im)
        et_ref[0:SSM_STATE, cols] = e_re.astype(BF16)
        et_ref[SSM_STATE:2 * SSM_STATE, cols] = e_im.astype(BF16)
        f_re, f_im = _cmul(ct_re[:, c:c + 1], ct_im[:, c:c + 1], v1_re, v1_im)
        f_ref[0:SSM_STATE, cols] = f_re.astype(BF16)
        f_ref[SSM_STATE:2 * SSM_STATE, cols] = (-f_im).astype(BF16)

    return taps, emit


def _ssm_param_views(log_dt, a_re, a_im, b_re, b_im, c_re, c_im):
    g, p, c = log_dt.shape[0], SSM_STATE, SSM_CH
    swap = lambda a: jnp.swapaxes(a, 1, 2)
    views = [
        log_dt.reshape(g, 1, 1),
        a_re.reshape(g, p, 1), a_im.reshape(g, p, 1),
        a_re.reshape(g, 1, p), a_im.reshape(g, 1, p),
        jnp.tile(a_re, (1, 2)).reshape(g, 1, 2 * p), jnp.tile(a_im, (1, 2)).reshape(g, 1, 2 * p),
        jnp.repeat(c_re, c, axis=1), jnp.repeat(c_im, c, axis=1),
        jnp.tile(swap(b_re), (1, c, 1)), jnp.tile(swap(b_im), (1, c, 1)),
        swap(c_re), swap(c_im),
        b_re, b_im,
    ]
    assert len(views) == N_SSM_PARAMS
    return views


SSM_N_TILE = 256


def _ssm_body(*refs, n_chunks, bsz):
    param_refs = refs[:N_SSM_PARAMS]
    u_ref, d_ref, y_ref = refs[N_SSM_PARAMS:N_SSM_PARAMS + 3]
    slots = refs[N_SSM_PARAMS + 3:-3]
    slot_a, slot_b = slots[:len(slots) // 2], slots[len(slots) // 2:]
    kt_scr, end_scr, h_scr = refs[-3:]
    rows = bsz * n_chunks
    slab_shape = (bsz, n_chunks // MIX_CHUNKS, MIX_CHUNKS, SSM_T)
    n_tiles = SSM_K // SSM_N_TILE
    step = pl.program_id(0)

    @pl.when(step == 0)
    def _zero_what_is_read_before_written():
        for ref in slot_b[:-1] + (slot_a[-1],):
            ref[...] = jnp.zeros(ref.shape, ref.dtype)

    def one_step(new, old):
        new_lhs, new_m, new_et, new_f, new_lam, new_raw = new
        old_lhs, old_m, old_et, old_f, old_lam, old_raw = old
        lam_a, lam_b = old_lam[0:1, :], old_lam[1:2, :]
        h = jnp.zeros((bsz, 2 * SSM_STATE), F32)
        taps, emit = _ssm_operator_emitter(param_refs, kt_scr, new_lam)

        end_scr[...] = _dot_nt(old_lhs[...], old_et[...])
        for j in range(n_tiles):
            cols = slice(j * SSM_N_TILE, (j + 1) * SSM_N_TILE)

            old_raw[:, cols] = _dot(old_lhs[...], old_m[:, cols])
            for n in range(j * n_chunks // n_tiles, (j + 1) * n_chunks // n_tiles):
                of_chunk = pl.ds(n, bsz, stride=n_chunks)
                h_scr[of_chunk, :] = h
                h = h * lam_a + pltpu.roll(h, SSM_STATE, 1) * lam_b + end_scr[of_chunk, :]

            if j == 0:
                taps()
            for c in range(j * SSM_CH // n_tiles, (j + 1) * SSM_CH // n_tiles):
                slab = slice(c * SSM_T, (c + 1) * SSM_T)
                of_channel = slice(c * MIX_CHUNKS, (c + 1) * MIX_CHUNKS)
                y_ref[:, :, of_channel, :] = jax.nn.gelu(new_raw[:, slab]).reshape(slab_shape)
                new_lhs[:, slab] = u_ref[:, :, of_channel, :].reshape(rows, SSM_T).astype(BF16)
                emit(c, new_m, new_et, new_f)

        h_in = h_scr[...].astype(BF16)
        for j in range(n_tiles):
            cols = slice(j * SSM_N_TILE, (j + 1) * SSM_N_TILE)
            old_raw[:, cols] += (_dot(h_in, old_f[:, cols])
                                 + d_ref[0, :, cols] * old_lhs[:, cols].astype(F32))

    parity = lax.rem(step, 2)
    pl.when(parity == 0)(lambda: one_step(slot_a, slot_b))
    pl.when(parity == 1)(lambda: one_step(slot_b, slot_a))


def _ssm(u, params, d_exp, *, layer):
    bsz, n_mix_tiles = u.shape[:2]
    n_chunks = n_mix_tiles * MIX_CHUNKS
    g = SSM_GROUPS
    rows = n_chunks * bsz
    lag = lambda k: (lambda s: (layer * g + jnp.clip(s - k, 0, g - 1), 0, 0))
    spec = lambda a, k: pl.BlockSpec((1,) + a.shape[1:], lag(k))
    act_spec = lambda k: pl.BlockSpec((bsz, n_mix_tiles, SSM_CH * MIX_CHUNKS, SSM_T),
                                      lambda s: (0, 0, jnp.clip(s - k, 0, g - 1), 0))
    slot = [pltpu.VMEM((rows, SSM_K), BF16),
            pltpu.VMEM((SSM_K, SSM_K), BF16),
            pltpu.VMEM((2 * SSM_STATE, SSM_K), BF16),
            pltpu.VMEM((2 * SSM_STATE, SSM_K), BF16),
            pltpu.VMEM((SUBLANES, 2 * SSM_STATE), F32),
            pltpu.VMEM((rows, SSM_K), F32)]
    return pl.pallas_call(
        functools.partial(_ssm_body, n_chunks=n_chunks, bsz=bsz),
        grid=(g + 2,),
        in_specs=[spec(a, 0) for a in params] + [act_spec(0), spec(d_exp, 1)],
        out_specs=act_spec(2),
        out_shape=jax.ShapeDtypeStruct(u.shape, F32),
        scratch_shapes=slot + slot + [pltpu.VMEM((SSM_CH * SSM_CH, SSM_T), F32),
                                      pltpu.VMEM((rows, 2 * SSM_STATE), F32),
                                      pltpu.VMEM((rows, 2 * SSM_STATE), F32)],
        compiler_params=_params(),
        name="ssm",
    )(*params, u, d_exp)


GLU_TILE = 256
assert GLU_TILE % SSM_CH == 0 and SSM_WIDTH % GLU_TILE == 0


def _mixout_body(at_ref, yg_ref, wg_ref, bg_ref, gs_ref, wo_ref, o_ref):
    n_parts = 2
    part_chunks = MIX_CHUNKS // n_parts
    part = part_chunks * SSM_T

    def glu_half(at, r0):
        return jnp.concatenate(
            [_dot(wg_ref[r0 + q * GLU_TILE:r0 + (q + 1) * GLU_TILE, q * GLU_TILE:(q + 1) * GLU_TILE],
                  at[q * GLU_TILE:(q + 1) * GLU_TILE])
             for q in range(SSM_WIDTH // GLU_TILE)], axis=0) + bg_ref[r0:r0 + SSM_WIDTH]

    glu = []
    for p in range(n_parts):
        at = jnp.concatenate([at_ref[pl.ds(p * part_chunks + j, SSM_WIDTH, stride=MIX_CHUNKS), :]
                              for j in range(part_chunks)], axis=1).astype(BF16)
        glu.append((glu_half(at, 0), glu_half(at, SSM_WIDTH)))
    o_ref[...] = _dot(yg_ref[...], wo_ref[SSM_WIDTH:, :])
    for p, (value, gate) in enumerate(glu):
        ot = value * jax.nn.sigmoid(gate)
        ms = jnp.mean(ot * ot, axis=0, keepdims=True)
        ot = (ot * lax.rsqrt(ms + EPS) * gs_ref[...]).astype(BF16)
        o_ref[p * part:(p + 1) * part, :] += _dot_tn(ot, wo_ref[:SSM_WIDTH, :])


def _mixout(a_t, y_gm, w_glu_t, b_glu, gain_ssm_out, w_out, *, layer):
    bsz, seq, _ = y_gm.shape
    tm = MIX_TOKENS
    tok = lambda b, i: (b, i, 0)
    return pl.pallas_call(
        _mixout_body,
        grid=(bsz, seq // tm),
        in_specs=[pl.BlockSpec((None, None, SSM_WIDTH * MIX_CHUNKS, SSM_T), lambda b, i: (b, i, 0, 0)),
                  pl.BlockSpec((None, tm, GM_WIDTH), tok)] + [
                  _layer_spec(a, layer) for a in (w_glu_t, b_glu, gain_ssm_out, w_out)],
        out_specs=pl.BlockSpec((None, tm, D_MODEL), tok),
        out_shape=jax.ShapeDtypeStruct((bsz, seq, D_MODEL), F32),
        compiler_params=_params(2),
        name="mixout",
    )(a_t, y_gm, w_glu_t, b_glu, gain_ssm_out, w_out)


def _glu_block_diag_t(glu_w, glu_b):
    g, c = SSM_GROUPS, SSM_CH
    rows = glu_w.reshape(g, c, 2, c).transpose(2, 0, 3, 1).reshape(2 * g * c, c)
    row_group = (jnp.arange(2 * g * c) // c) % g
    col = jnp.arange(g * c)
    dense = jnp.where(row_group[:, None] == (col // c)[None, :], rows[:, col % c], 0.0)
    bias = glu_b.reshape(g, 2, c).transpose(1, 0, 2).reshape(2 * g * c, 1)
    return dense.astype(BF16), bias


def _tile(n, want, quantum):
    t = min(want, n)
    assert n % t == 0 and t % quantum == 0
    return t


def kernel(x, norm_ffn1, ffn1_w_in, ffn1_w_out, norm_mix, mix_w_in, ssm_a_re, ssm_a_im, ssm_log_dt, ssm_b_re, ssm_b_im, ssm_c_re, ssm_c_im, ssm_d, ssm_glu_w, ssm_glu_b, gm_v_gain, gm_w_s, gm_b_s, gain_ssm_out, gain_gm_out, mix_w_out, norm_ffn2, ffn2_w_in, ffn2_w_out, norm_final):
    bsz, seq, _ = x.shape
    depth = norm_ffn1.shape[0]
    n = bsz * seq
    assert seq % MIX_TOKENS == 0 and bsz % SUBLANES == 0
    tm_ffn = _tile(n, FFN_TOKENS, SUBLANES)
    g = SSM_GROUPS
    gain_final = norm_final.reshape(1, D_MODEL)
    flat = lambda a: a.reshape(n, D_MODEL)
    seqs = lambda a: a.reshape(bsz, seq, D_MODEL)

    ffn1_in, ffn1_out = ffn1_w_in.astype(BF16), ffn1_w_out.astype(BF16)
    ffn2_in, ffn2_out = ffn2_w_in.astype(BF16), ffn2_w_out.astype(BF16)
    mix_in, mix_out = mix_w_in.astype(BF16), mix_w_out.astype(BF16)
    mix_in_ssm_t = jnp.swapaxes(mix_in[:, :, :SSM_WIDTH], 1, 2)

    row = lambda a: a.reshape(depth, 1, -1)
    col = lambda a: a.reshape(depth, -1, 1)
    all_groups = lambda a: a.reshape((depth * g,) + a.shape[2:])
    ssm_params = _ssm_param_views(*(all_groups(a) for a in (
        ssm_log_dt, ssm_a_re, ssm_a_im, ssm_b_re, ssm_b_im, ssm_c_re, ssm_c_im)))
    d_exp = jnp.repeat(all_groups(ssm_d), SSM_T, axis=1).reshape(depth * g, 1, SSM_K)
    w_glu_t, b_glu = jax.vmap(_glu_block_diag_t)(ssm_glu_w, ssm_glu_b)
    b_s_cols = gm_b_s.reshape(depth, GM_HEADS, GM_CHUNK, 1)

    h = flat(x)
    for l in range(depth):
        h = _ffn(h, None, row(norm_ffn1), ffn1_in, ffn1_out, gain_final,
                 layer=l, final_norm=False, tm=tm_ffn)

        u_t, y_gm = _mixin(seqs(h), row(norm_mix), mix_in_ssm_t, mix_in, row(gm_v_gain), gm_w_s,
                           b_s_cols, row(gain_gm_out), layer=l)

        a_t = _ssm(u_t, ssm_params, d_exp, layer=l)

        delta = _mixout(a_t, y_gm, w_glu_t, b_glu, col(gain_ssm_out), mix_out, layer=l)

        h = _ffn(h, flat(delta), row(norm_ffn2), ffn2_in, ffn2_out, gain_final,
                 layer=l, final_norm=(l == depth - 1), tm=tm_ffn)
    return seqs(h)
```
